```python
import math
import jax, jax.numpy as jnp
from jax import lax
import numpy as np

D_MODEL = 1024
BATCH = 32
SEQ = 2048
DEPTH = 4
DEC_BATCH = 1
DEC_SEQ = 16384
PAST_LEN = 128

HEAD_DIM = 64
GRID_W = 64
NA_HEADS = 16
NA_WIDTH = NA_HEADS * HEAD_DIM
NA_WIN_ROWS = 8
NA_WIN_COLS = 16
SWA_Q_HEADS = 16
SWA_KV_HEADS = 4
SWA_WIDTH = SWA_Q_HEADS * HEAD_DIM
SWA_KV_WIDTH = SWA_KV_HEADS * HEAD_DIM
SWA_WINDOW = 128
SWA_BLOCK = 128
ROPE_THETA = 10000.0
SSD_HEADS = 16
SSD_HEAD_DIM = 64
SSD_INNER = SSD_HEADS * SSD_HEAD_DIM
SSD_GROUPS = 4
SSD_STATE = 128
SSD_CONV = 4
SSD_CONV_DIM = SSD_INNER + 2 * SSD_GROUPS * SSD_STATE
SSD_CHUNK = 128
N_BRANCHES = 3
N_EXPERTS = 16
EXPERT_FF = 2 * D_MODEL
CAPACITY_FACTOR = 2
NORM_EPS = 1e-6
NEG_INF = -1e30
PROJ_SPLITS = (NA_WIDTH, NA_WIDTH, NA_WIDTH,
               SWA_WIDTH, SWA_KV_WIDTH, SWA_KV_WIDTH,
               SSD_INNER, SSD_CONV_DIM, 2 * SSD_HEADS,
               N_BRANCHES * D_MODEL)
PROJ_COLS = sum(PROJ_SPLITS)

kernel_name = 'hybrid_na_swa_ssd_ec_encoder'


def rmsnorm(x, w):
    x32 = x.astype(jnp.float32)
    y = x32 * lax.rsqrt(jnp.mean(x32 * x32, axis=-1, keepdims=True) + NORM_EPS)
    return (y * w.astype(jnp.float32)).astype(x.dtype)


def rotary(x):
    seq, hd = x.shape[1], x.shape[-1]
    half = hd // 2
    inv = ROPE_THETA ** (-jnp.arange(half, dtype=jnp.float32) / half)
    ang = jnp.arange(seq, dtype=jnp.float32)[:, None] * inv[None, :]
    cos = jnp.cos(ang)[None, :, None, :]
    sin = jnp.sin(ang)[None, :, None, :]
    x32 = x.astype(jnp.float32)
    x1, x2 = x32[..., :half], x32[..., half:]
    return jnp.concatenate([x1 * cos - x2 * sin, x2 * cos + x1 * sin], axis=-1).astype(x.dtype)


def split_columns(proj):
    offsets = []
    acc = 0
    for w in PROJ_SPLITS[:-1]:
        acc += w
        offsets.append(acc)
    return jnp.split(proj, offsets, axis=-1)


def neighborhood_attention(q, k, v, rel_bias):
    bsz, seq, nh, hd = q.shape
    rows = seq // GRID_W
    win_r = min(NA_WIN_ROWS, rows)
    scale = hd ** -0.5
    q_g = q.reshape(bsz, rows, GRID_W, nh, hd)
    k_g = k.reshape(bsz, rows, GRID_W, nh, hd)
    v_g = v.reshape(bsz, rows, GRID_W, nh, hd)
    col = jnp.arange(GRID_W)
    col_start = jnp.clip(col - NA_WIN_COLS // 2, 0, GRID_W - NA_WIN_COLS)
    col_idx = col_start[:, None] + jnp.arange(NA_WIN_COLS)[None, :]
    col_rel = col_idx - col[:, None] + NA_WIN_COLS - 1
    bias_cols = rel_bias.astype(jnp.float32)[:, :, col_rel]

    def row_block(i):
        b = i // rows
        r = i % rows
        r0 = jnp.clip(r - win_r // 2, 0, rows - win_r)
        q_r = lax.dynamic_slice(q_g, (b, r, 0, 0, 0), (1, 1, GRID_W, nh, hd))[0, 0]
        k_r = lax.dynamic_slice(k_g, (b, r0, 0, 0, 0), (1, win_r, GRID_W, nh, hd))[0][:, col_idx]
        v_r = lax.dynamic_slice(v_g, (b, r0, 0, 0, 0), (1, win_r, GRID_W, nh, hd))[0][:, col_idx]
        row_rel = r0 + jnp.arange(win_r) - r + NA_WIN_ROWS - 1
        bias = bias_cols[:, row_rel].transpose(0, 2, 1, 3)
        s = jnp.einsum('qhd,iqjhd->hqij', q_r, k_r, preferred_element_type=jnp.float32) * scale + bias
        p = jax.nn.softmax(s.reshape(nh, GRID_W, win_r * NA_WIN_COLS), axis=-1).reshape(s.shape)
        return jnp.einsum('hqij,iqjhd->qhd', p.astype(v.dtype), v_r)

    out = lax.map(row_block, jnp.arange(bsz * rows))
    return out.reshape(bsz, seq, nh * hd)


def sliding_window_attention(q, k, v, sink):
    bsz, seq, nq, hd = q.shape
    nkv = k.shape[2]
    grp = nq // nkv
    scale = hd ** -0.5
    nblk = seq // SWA_BLOCK
    span = SWA_BLOCK + 2 * SWA_WINDOW
    q_b = q.reshape(bsz, nblk, SWA_BLOCK, nkv, grp, hd)
    pad = ((0, 0), (SWA_WINDOW, SWA_WINDOW), (0, 0), (0, 0))
    k_p = jnp.pad(k, pad)
    v_p = jnp.pad(v, pad)
    rel = jnp.arange(span)[None, :] - SWA_WINDOW - jnp.arange(SWA_BLOCK)[:, None]
    in_band = jnp.abs(rel) <= SWA_WINDOW
    sink_l = sink.astype(jnp.float32).reshape(1, nkv, grp, 1, 1)

    def block(bi):
        start = bi * SWA_BLOCK
        q_i = lax.dynamic_index_in_dim(q_b, bi, axis=1, keepdims=False)
        k_i = lax.dynamic_slice_in_dim(k_p, start, span, axis=1)
        v_i = lax.dynamic_slice_in_dim(v_p, start, span, axis=1)
        key_pos = start - SWA_WINDOW + jnp.arange(span)
        valid = in_band & ((key_pos >= 0) & (key_pos < seq))[None, :]
        s = jnp.einsum('bqkgd,bskd->bkgqs', q_i, k_i, preferred_element_type=jnp.float32) * scale
        s = jnp.where(valid, s, NEG_INF)
        m = jnp.maximum(s.max(axis=-1, keepdims=True), sink_l)
        e = jnp.exp(s - m)
        p = e / (e.sum(axis=-1, keepdims=True) + jnp.exp(sink_l - m))
        return jnp.einsum('bkgqs,bskd->bqkgd', p.astype(v.dtype), v_i)

    out = lax.map(block, jnp.arange(nblk))
    return jnp.moveaxis(out, 0, 1).reshape(bsz, seq, nq * hd)


def ssd_chunked_scan(x, dt, a, bm, cm):
    bsz, seq, nh, hp = x.shape
    ng, ns = bm.shape[2], bm.shape[3]
    hpg = nh // ng
    nc = seq // SSD_CHUNK
    ln = SSD_CHUNK
    xdt = (x * dt[..., None]).reshape(bsz, nc, ln, ng, hpg, hp)
    a_dt = (dt * a).reshape(bsz, nc, ln, ng, hpg).transpose(0, 3, 4, 1, 2)
    bm = bm.reshape(bsz, nc, ln, ng, ns)
    cm = cm.reshape(bsz, nc, ln, ng, ns)
    a_cum = jnp.cumsum(a_dt, axis=-1)
    causal = jnp.tril(jnp.ones((ln, ln), dtype=bool))
    seg = a_cum[..., :, None] - a_cum[..., None, :]
    decay = jnp.exp(jnp.where(causal, seg, -jnp.inf))
    cb = jnp.einsum('bclgn,bcsgn->bgcls', cm, bm)
    y_diag = jnp.einsum('bgcls,bgjcls,bcsgjp->bclgjp', cb, decay, xdt)
    decay_states = jnp.exp(a_cum[..., -1:] - a_cum)
    states = jnp.einsum('bcsgn,bgjcs,bcsgjp->bcgjpn', bm, decay_states, xdt)
    chunk_decay = jnp.exp(a_cum[..., -1])

    def carry_step(h, inp):
        st, dc = inp
        return h * dc[..., None, None] + st, h

    h0 = jnp.zeros((bsz, ng, hpg, hp, ns), jnp.float32)
    _, prev = lax.scan(carry_step, h0, (jnp.moveaxis(states, 1, 0), jnp.moveaxis(chunk_decay, -1, 0)))
    prev = jnp.moveaxis(prev, 0, 1)
    y_off = jnp.einsum('bclgn,bcgjpn,bgjcl->bclgjp', cm, prev, jnp.exp(a_cum))
    return (y_diag + y_off).reshape(bsz, seq, nh, hp)


def ssd_mixer(z, xbc, dt_raw, conv_w, conv_b, dt_bias, a_log, d_skip, norm_w):
    bsz, seq, _ = z.shape
    pad_l = SSD_CONV // 2
    pad_r = SSD_CONV - 1 - pad_l
    xbc = lax.conv_general_dilated(xbc, conv_w[:, None, :].astype(xbc.dtype), (1,), [(pad_l, pad_r)],
                                   dimension_numbers=('NWC', 'WIO', 'NWC'), feature_group_count=SSD_CONV_DIM)
    xbc = jax.nn.silu(xbc + conv_b.astype(xbc.dtype))
    xs, bm, cm = jnp.split(xbc, [SSD_INNER, SSD_INNER + SSD_GROUPS * SSD_STATE], axis=-1)
    xs = xs.reshape(bsz, seq, SSD_HEADS, SSD_HEAD_DIM).astype(jnp.float32)
    bm = bm.reshape(bsz, seq, SSD_GROUPS, SSD_STATE).astype(jnp.float32)
    cm = cm.reshape(bsz, seq, SSD_GROUPS, SSD_STATE).astype(jnp.float32)
    dt_all = jax.nn.softplus(dt_raw.reshape(bsz, seq, 2, SSD_HEADS).astype(jnp.float32) + dt_bias.astype(jnp.float32))
    a_all = -jnp.exp(a_log.astype(jnp.float32))
    y_fwd = ssd_chunked_scan(xs, dt_all[:, :, 0], a_all[0], bm, cm)
    flip = lambda t: jnp.flip(t, axis=1)
    y_bwd = flip(ssd_chunked_scan(flip(xs), flip(dt_all[:, :, 1]), a_all[1], flip(bm), flip(cm)))
    y = y_fwd + y_bwd + d_skip.astype(jnp.float32)[:, None] * xs
    y = y.reshape(bsz, seq, SSD_INNER) * jax.nn.silu(z.astype(jnp.float32))
    y = y.reshape(bsz, seq, SSD_GROUPS, SSD_INNER // SSD_GROUPS)
    y = y * lax.rsqrt(jnp.mean(y * y, axis=-1, keepdims=True) + NORM_EPS)
    y = y.reshape(bsz, seq, SSD_INNER) * norm_w.astype(jnp.float32)
    return y.astype(z.dtype)


def mixer_layer(x, norm_w, w_in, na_q_norm, na_k_norm, na_rel_bias, swa_q_norm, swa_k_norm, swa_sink,
                conv_w, conv_b, dt_bias, a_log, d_skip, ssd_norm, w_na, w_swa, w_ssd, w_out):
    bsz, seq, _ = x.shape
    h = rmsnorm(x, norm_w)
    proj = jnp.einsum('bsd,dc->bsc', h, w_in)
    (na_q, na_k, na_v, swa_q, swa_k, swa_v, ssd_z, ssd_xbc, ssd_dt, gate_logits) = split_columns(proj)
    heads = lambda t, n: t.reshape(bsz, seq, n, HEAD_DIM)
    o_na = neighborhood_attention(rmsnorm(heads(na_q, NA_HEADS), na_q_norm),
                                  rmsnorm(heads(na_k, NA_HEADS), na_k_norm),
                                  heads(na_v, NA_HEADS), na_rel_bias)
    o_swa = sliding_window_attention(rotary(rmsnorm(heads(swa_q, SWA_Q_HEADS), swa_q_norm)),
                                     rotary(rmsnorm(heads(swa_k, SWA_KV_HEADS), swa_k_norm)),
                                     heads(swa_v, SWA_KV_HEADS), swa_sink)
    o_ssd = ssd_mixer(ssd_z, ssd_xbc, ssd_dt, conv_w, conv_b, dt_bias, a_log, d_skip, ssd_norm)
    gates = jax.nn.sigmoid(gate_logits.astype(jnp.float32)).reshape(bsz, seq, N_BRANCHES, D_MODEL).astype(x.dtype)
    merged = (gates[:, :, 0] * (o_na @ w_na)
              + gates[:, :, 1] * (o_swa @ w_swa)
              + gates[:, :, 2] * (o_ssd @ w_ssd))
    return x + merged @ w_out


def moe_layer(x, norm_w, router_w, w_gate, w_up, w_down):
    bsz, seq, d = x.shape
    h = rmsnorm(x, norm_w).reshape(bsz * seq, d)
    ntok = h.shape[0]
    cap = CAPACITY_FACTOR * ntok // N_EXPERTS
    affinity = jax.nn.softmax(jnp.einsum('td,de->te', h, router_w, preferred_element_type=jnp.float32), axis=-1)
    top_val, top_idx = lax.top_k(affinity.T, cap)
    xe = h[top_idx]

    def expert(args):
        xe_i, wg, wu, wd = args
        return (jax.nn.silu(xe_i @ wg) * (xe_i @ wu)) @ wd

    ye = lax.map(expert, (xe, w_gate, w_up, w_down))
    weighted = ye * top_val[..., None].astype(ye.dtype)
    out = jnp.zeros_like(h).at[top_idx.reshape(-1)].add(weighted.reshape(-1, d))
    return x + out.reshape(bsz, seq, d)


def setup_inputs(seed: int = 0) -> dict:
    key = jax.random.key(seed)
    ks = jax.random.split(key, 26)
    f32 = jnp.float32

    def normal(k, shape, scale):
        return jax.random.normal(k, shape, f32) * scale

    def gain(k, shape):
        return 1.0 + 0.01 * jax.random.normal(k, shape, f32)

    dt_init = jnp.exp(jax.random.uniform(ks[13], (DEPTH, 2, SSD_HEADS), f32, math.log(1e-3), math.log(1e-1)))
    return {
        'x_prompt': normal(ks[0], (BATCH, SEQ, D_MODEL), 1.0),
        'x_sample': normal(ks[1], (DEC_BATCH, DEC_SEQ, D_MODEL), 1.0),
        'norm_mix': gain(ks[2], (DEPTH, D_MODEL)),
        'w_in': normal(ks[3], (DEPTH, D_MODEL, PROJ_COLS), D_MODEL ** -0.5),
        'na_q_norm': gain(ks[4], (DEPTH, HEAD_DIM)),
        'na_k_norm': gain(ks[5], (DEPTH, HEAD_DIM)),
        'na_rel_bias': normal(ks[6], (DEPTH, NA_HEADS, 2 * NA_WIN_ROWS - 1, 2 * NA_WIN_COLS - 1), 0.1),
        'swa_q_norm': gain(ks[7], (DEPTH, HEAD_DIM)),
        'swa_k_norm': gain(ks[8], (DEPTH, HEAD_DIM)),
        'swa_sink': normal(ks[9], (DEPTH, SWA_Q_HEADS), 1.0),
        'ssd_conv_w': normal(ks[10], (DEPTH, SSD_CONV, SSD_CONV_DIM), SSD_CONV ** -0.5),
        'ssd_conv_b': normal(ks[11], (DEPTH, SSD_CONV_DIM), 0.01),
        'ssd_dt_bias': dt_init + jnp.log(-jnp.expm1(-dt_init)),
        'ssd_a_log': jnp.log(jax.random.uniform(ks[12], (DEPTH, 2, SSD_HEADS), f32, 1.0, 16.0)),
        'ssd_d': gain(ks[14], (DEPTH, SSD_HEADS)),
        'ssd_norm': gain(ks[15], (DEPTH, SSD_INNER)),
        'w_branch_na': normal(ks[16], (DEPTH, NA_WIDTH, D_MODEL), NA_WIDTH ** -0.5),
        'w_branch_swa': normal(ks[17], (DEPTH, SWA_WIDTH, D_MODEL), SWA_WIDTH ** -0.5),
        'w_branch_ssd': normal(ks[18], (DEPTH, SSD_INNER, D_MODEL), SSD_INNER ** -0.5),
        'w_out': normal(ks[19], (DEPTH, D_MODEL, D_MODEL), D_MODEL ** -0.5),
        'norm_ffn': gain(ks[20], (DEPTH, D_MODEL)),
        'router_w': normal(ks[21], (DEPTH, D_MODEL, N_EXPERTS), D_MODEL ** -0.5),
        'expert_w_gate': normal(ks[22], (DEPTH, N_EXPERTS, D_MODEL, EXPERT_FF), D_MODEL ** -0.5),
        'expert_w_up': normal(ks[23], (DEPTH, N_EXPERTS, D_MODEL, EXPERT_FF), D_MODEL ** -0.5),
        'expert_w_down': normal(ks[24], (DEPTH, N_EXPERTS, EXPERT_FF, D_MODEL), EXPERT_FF ** -0.5),
    }


def reference(x_prompt, x_sample, norm_mix, w_in, na_q_norm, na_k_norm, na_rel_bias, swa_q_norm, swa_k_norm,
              swa_sink, ssd_conv_w, ssd_conv_b, ssd_dt_bias, ssd_a_log, ssd_d, ssd_norm, w_branch_na,
              w_branch_swa, w_branch_ssd, w_out, norm_ffn, router_w, expert_w_gate, expert_w_up, expert_w_down):
    mix_params = (norm_mix, w_in, na_q_norm, na_k_norm, na_rel_bias, swa_q_norm, swa_k_norm, swa_sink,
                  ssd_conv_w, ssd_conv_b, ssd_dt_bias, ssd_a_log, ssd_d, ssd_norm,
                  w_branch_na, w_branch_swa, w_branch_ssd, w_out)
    ffn_params = (norm_ffn, router_w, expert_w_gate, expert_w_up, expert_w_down)

    def run_trunk(x):
        for layer in range(DEPTH):
            x = mixer_layer(x, *[p[layer] for p in mix_params])
            x = moe_layer(x, *[p[layer] for p in ffn_params])
        return x

    y_prompt = run_trunk(x_prompt)
    y_sample = run_trunk(x_sample)
    return (y_prompt, y_sample)
```

```python
import functools
import math

import jax
import jax.numpy as jnp
from jax import lax
from jax.experimental import pallas as pl
from jax.experimental.pallas import tpu as pltpu

F32 = jnp.float32
BF16 = jnp.bfloat16
I32 = jnp.int32

D_MODEL = 1024
HEAD_DIM = 64
GRID_W = 64
NA_HEADS = 16
NA_WIN_ROWS = 8
NA_WIN_COLS = 16
NA_QROWS = 4
NA_KROWS = 12
SWA_Q_HEADS = 16
SWA_KV_HEADS = 4
SWA_WINDOW = 128
SWA_BLOCK = 128
ROPE_THETA = 10000.0
SSD_HEADS = 16
SSD_GROUPS = 4
SSD_STATE = 128
SSD_CONV = 4
SSD_CHUNK = 128
SSD_INNER = 1024
N_EXPERTS = 16
EXPERT_FF = 2048
CAPACITY_FACTOR = 2
NORM_EPS = 1e-6
NEG_INF = -1e30
LANES = 128
VMEM_LIMIT = 56 * 1024 * 1024


def _cparams(sem):
    return pltpu.CompilerParams(dimension_semantics=sem, vmem_limit_bytes=VMEM_LIMIT)


def _rmsnorm(x, w):
    return x * lax.rsqrt(jnp.mean(x * x, axis=-1, keepdims=True) + NORM_EPS) * w


def _lo_mask(shape):
    return lax.broadcasted_iota(I32, shape, len(shape) - 1) < HEAD_DIM


def _headnorm2(x, w):
    lo = _lo_mask(x.shape)
    x2 = x * x
    sa = jnp.sum(jnp.where(lo, x2, 0.0), axis=-1, keepdims=True)
    sb = jnp.sum(jnp.where(lo, 0.0, x2), axis=-1, keepdims=True)
    ms = jnp.where(lo, sa, sb) * (1.0 / HEAD_DIM)
    return x * lax.rsqrt(ms + NORM_EPS) * w


def _stack_halves(x):
    lo = _lo_mask(x.shape)
    zero = jnp.zeros_like(x)
    return jnp.concatenate([jnp.where(lo, x, zero), jnp.where(lo, zero, x)], axis=0)


def _dot(a, b):
    return jnp.dot(a, b, preferred_element_type=F32)


def _dot_nt(a, b):
    return lax.dot_general(a, b, (((1,), (1,)), ((), ())), preferred_element_type=F32)


def _dot_tn(a, b):
    return lax.dot_general(a, b, (((0,), (0,)), ((), ())), preferred_element_type=F32)


def _split2(x):
    hi = x.astype(BF16)
    lo = (x - hi.astype(F32)).astype(BF16)
    return hi, lo


def _split3(x):
    a = x.astype(BF16)
    r = x - a.astype(F32)
    b = r.astype(BF16)
    c = (r - b.astype(F32)).astype(BF16)
    return a, b, c


def _norm_proj_kernel(*refs, n_out):
    x_ref, nw_ref = refs[0], refs[1]
    w_refs = refs[2:2 + n_out]
    o_refs = refs[2 + n_out:2 + 2 * n_out]
    h = _rmsnorm(x_ref[...], nw_ref[...]).astype(BF16)
    for w_ref, o_ref in zip(w_refs, o_refs):
        o_ref[...] = _dot(h, w_ref[...]).astype(o_ref.dtype)


def norm_proj(x2d, nw, weights, out_dtypes, tm=512):
    t, d = x2d.shape
    tm = min(tm, t)
    n_out = len(weights)
    return pl.pallas_call(
        functools.partial(_norm_proj_kernel, n_out=n_out),
        grid=(t // tm,),
        in_specs=[pl.BlockSpec((tm, d), lambda i: (i, 0)), pl.BlockSpec((1, d), lambda i: (0, 0))]
        + [pl.BlockSpec(w.shape, lambda i: (0, 0)) for w in weights],
        out_specs=[pl.BlockSpec((tm, w.shape[1]), lambda i: (i, 0)) for w in weights],
        out_shape=[jax.ShapeDtypeStruct((t, w.shape[1]), dt) for w, dt in zip(weights, out_dtypes)],
        compiler_params=_cparams(("parallel",)),
        name="norm_proj",
    )(x2d, nw, *weights)


def _na_kernel(q_ref, k0_ref, k1_ref, k2_ref, v0_ref, v1_ref, v2_ref, qn_ref, kn_ref, tab_ref, o_ref):
    g = pl.program_id(2)
    ng = pl.num_programs(2)
    variant = jnp.where(g == 0, 0, jnp.where(g == ng - 1, 2, 1))
    nk = NA_KROWS * GRID_W
    q = _headnorm2(q_ref[0].astype(F32), qn_ref[...]) * (HEAD_DIM ** -0.5)
    kw = jnp.concatenate([k0_ref[0], k1_ref[0], k2_ref[0]], axis=0).astype(F32)
    kw = _headnorm2(kw, kn_ref[...]).astype(BF16)
    s = _dot_nt(q.astype(BF16), _stack_halves(kw)) + tab_ref[0, variant]
    sa, sb = s[:, :nk], s[:, nk:]
    pa = jnp.exp(sa - jnp.max(sa, axis=-1, keepdims=True))
    pb = jnp.exp(sb - jnp.max(sb, axis=-1, keepdims=True))
    la = jnp.sum(pa, axis=-1, keepdims=True)
    lb = jnp.sum(pb, axis=-1, keepdims=True)
    p = jnp.concatenate([pa, pb], axis=1).astype(BF16)
    vw = jnp.concatenate([v0_ref[0], v1_ref[0], v2_ref[0]], axis=0)
    o = _dot(p, _stack_halves(vw))
    o_ref[0] = (o * jnp.where(_lo_mask(o.shape), 1.0 / la, 1.0 / lb)).astype(o_ref.dtype)


def na_bias_table(rel_bias):
    nh = rel_bias.shape[0]
    offs = jnp.array([0, NA_QROWS, 2 * NA_QROWS], I32)
    r0w = jnp.array([[0, 0, 0, 0], [0, 1, 2, 3], [4, 4, 4, 4]], I32)
    j = jnp.arange(NA_QROWS, dtype=I32)
    i = jnp.arange(NA_KROWS, dtype=I32)
    c = jnp.arange(GRID_W, dtype=I32)
    qrow = offs[:, None] + j[None, :]
    row_rel = i[None, None, :] - qrow[:, :, None] + NA_WIN_ROWS - 1
    row_ok = (i[None, None, :] >= r0w[:, :, None]) & (i[None, None, :] < r0w[:, :, None] + NA_WIN_ROWS)
    c0 = jnp.clip(c - NA_WIN_COLS // 2, 0, GRID_W - NA_WIN_COLS)
    col_rel = c[None, :] - c[:, None] + NA_WIN_COLS - 1
    col_ok = (c[None, :] >= c0[:, None]) & (c[None, :] < c0[:, None] + NA_WIN_COLS)
    rr = jnp.clip(row_rel, 0, 2 * NA_WIN_ROWS - 2)
    cr = jnp.clip(col_rel, 0, 2 * NA_WIN_COLS - 2)
    b = rel_bias.astype(F32)[:, rr[:, :, None, :, None], cr[None, None, :, None, :]]
    ok = row_ok[:, :, None, :, None] & col_ok[None, None, :, None, :]
    b = jnp.where(ok[None], b, NEG_INF)
    b = b.reshape(nh // 2, 2, 3, NA_QROWS * GRID_W, NA_KROWS * GRID_W)
    return jnp.concatenate([b[:, 0], b[:, 1]], axis=-1)


def neighborhood_attention(qkv, qn, kn, tab):
    bsz, seq, _ = qkv.shape
    rows = seq // GRID_W
    assert rows % NA_QROWS == 0 and rows >= NA_KROWS + NA_QROWS
    ng = rows // NA_QROWS
    tq = NA_QROWS * GRID_W
    nhp = NA_HEADS // 2

    def kv_spec(col0, j):
        return pl.BlockSpec((1, tq, LANES),
                            lambda hp, b, g: (b, jnp.clip(g - 1, 0, ng - 3) + j, col0 + hp))

    return pl.pallas_call(
        _na_kernel,
        grid=(nhp, bsz, ng),
        in_specs=[pl.BlockSpec((1, tq, LANES), lambda hp, b, g: (b, g, hp))]
        + [kv_spec(nhp, j) for j in range(3)] + [kv_spec(2 * nhp, j) for j in range(3)]
        + [pl.BlockSpec((1, LANES), lambda hp, b, g: (0, 0))] * 2
        + [pl.BlockSpec((1, 3, tq, 2 * NA_KROWS * GRID_W), lambda hp, b, g: (hp, 0, 0, 0))],
        out_specs=pl.BlockSpec((1, tq, LANES), lambda hp, b, g: (b, g, hp)),
        out_shape=jax.ShapeDtypeStruct((bsz, seq, NA_HEADS * HEAD_DIM), BF16),
        compiler_params=_cparams(("parallel", "parallel", "parallel")),
        name="neighborhood_attention",
    )(qkv, qkv, qkv, qkv, qkv, qkv, qkv, qn, kn, tab)


def _rope(x, cos, sin_signed):
    first = (lax.broadcasted_iota(I32, x.shape, 1) % HEAD_DIM) < HEAD_DIM // 2
    partner = jnp.where(first, pltpu.roll(x, LANES - HEAD_DIM // 2, axis=1), pltpu.roll(x, HEAD_DIM // 2, axis=1))
    return x * cos + partner * sin_signed


def _swa_kernel(sink_ref, q_ref, kp_ref, kc_ref, kx_ref, vp_ref, vc_ref, vx_ref,
                cp_ref, cc_ref, cx_ref, sp_ref, sc_ref, sx_ref, qn_ref, kn_ref, o_ref, *, seq):
    kvh = pl.program_id(1)
    i = pl.program_id(2)
    blk = SWA_BLOCK
    span = 3 * blk
    q = q_ref[0].astype(F32)
    q2 = jnp.concatenate([q[:, :LANES], q[:, LANES:]], axis=0)
    cos_q = jnp.concatenate([cc_ref[...], cc_ref[...]], axis=0)
    sin_q = jnp.concatenate([sc_ref[...], sc_ref[...]], axis=0)
    q2 = (_rope(_headnorm2(q2, qn_ref[...]), cos_q, sin_q) * (HEAD_DIM ** -0.5)).astype(BF16)

    keep = _lo_mask((span, LANES)) == (kvh % 2 == 0)

    def dup(x):
        return jnp.where(keep, x, pltpu.roll(x, HEAD_DIM, axis=1))

    kraw = jnp.concatenate([kp_ref[0], kc_ref[0], kx_ref[0]], axis=0).astype(F32)
    cos_k = jnp.concatenate([cp_ref[...], cc_ref[...], cx_ref[...]], axis=0)
    sin_k = jnp.concatenate([sp_ref[...], sc_ref[...], sx_ref[...]], axis=0)
    kd = _rope(_headnorm2(dup(kraw), kn_ref[...]), cos_k, sin_k).astype(BF16)
    s = _dot_nt(q2, _stack_halves(kd))

    qpos = i * blk + lax.broadcasted_iota(I32, (2 * blk, span), 0) % blk
    kpos = (i - 1) * blk + lax.broadcasted_iota(I32, (2 * blk, span), 1)
    valid = (jnp.abs(kpos - qpos) <= SWA_WINDOW) & (kpos >= 0) & (kpos < seq)
    upper = lax.broadcasted_iota(I32, (2 * blk, 1), 0) < blk
    h0 = 4 * kvh
    sink_a = jnp.where(upper, sink_ref[h0], sink_ref[h0 + 2])
    sink_b = jnp.where(upper, sink_ref[h0 + 1], sink_ref[h0 + 3])

    def softmax_part(sc, sink):
        sc = jnp.where(valid, sc, NEG_INF)
        m = jnp.maximum(jnp.max(sc, axis=-1, keepdims=True), sink)
        e = jnp.exp(sc - m)
        return e, jnp.sum(e, axis=-1, keepdims=True) + jnp.exp(sink - m)

    ea, la = softmax_part(s[:, :span], sink_a)
    eb, lb = softmax_part(s[:, span:], sink_b)
    p = jnp.concatenate([ea, eb], axis=1).astype(BF16)
    vraw = jnp.concatenate([vp_ref[0], vc_ref[0], vx_ref[0]], axis=0).astype(F32)
    o = _dot(p, _stack_halves(dup(vraw).astype(BF16)))
    o = o * jnp.where(_lo_mask(o.shape), 1.0 / la, 1.0 / lb)
    o_ref[0] = jnp.concatenate([o[:blk], o[blk:]], axis=1).astype(o_ref.dtype)


def rope_tables(seq):
    half = HEAD_DIM // 2
    inv = ROPE_THETA ** (-jnp.arange(half, dtype=F32) / half)
    ang = jnp.arange(seq, dtype=F32)[:, None] * inv[None, :]
    cos, sin = jnp.cos(ang), jnp.sin(ang)
    return jnp.tile(cos, (1, 4)), jnp.tile(jnp.concatenate([-sin, sin], axis=1), (1, 2))


def sliding_window_attention(qkv, sink, qn, kn, cos, sin):
    bsz, seq, _ = qkv.shape
    blk = SWA_BLOCK
    nblk = seq // blk
    qw = 4 * HEAD_DIM
    kcol0 = SWA_Q_HEADS * HEAD_DIM // LANES
    vcol0 = kcol0 + SWA_KV_HEADS * HEAD_DIM // LANES

    def seq_idx(i, j):
        return jnp.clip(i + j - 1, 0, nblk - 1)

    def kv_spec(col0, j):
        return pl.BlockSpec((1, blk, LANES), lambda b, kvh, i: (b, seq_idx(i, j), col0 + kvh // 2))

    def tab_spec(j):
        return pl.BlockSpec((blk, LANES), lambda b, kvh, i: (seq_idx(i, j), 0))

    return pl.pallas_call(
        functools.partial(_swa_kernel, seq=seq),
        grid=(bsz, SWA_KV_HEADS, nblk),
        in_specs=[pl.BlockSpec(memory_space=pltpu.SMEM),
                  pl.BlockSpec((1, blk, qw), lambda b, kvh, i: (b, i, kvh))]
        + [kv_spec(kcol0, j) for j in range(3)] + [kv_spec(vcol0, j) for j in range(3)]
        + [tab_spec(j) for j in range(3)] + [tab_spec(j) for j in range(3)]
        + [pl.BlockSpec((1, LANES), lambda b, kvh, i: (0, 0))] * 2,
        out_specs=pl.BlockSpec((1, blk, qw), lambda b, kvh, i: (b, i, kvh)),
        out_shape=jax.ShapeDtypeStruct((bsz, seq, SWA_Q_HEADS * HEAD_DIM), BF16),
        compiler_params=_cparams(("parallel", "parallel", "parallel")),
        name="sliding_window_attention",
    )(sink, qkv, qkv, qkv, qkv, qkv, qkv, qkv, cos, cos, cos, sin, sin, sin, qn, kn)


def _conv_silu(main_ref, prev_ref, next_ref, w_ref, b_ref, has_prev, has_next):
    xm = main_ref[0].astype(F32)
    n = xm.shape[0]
    pv = prev_ref[0].astype(F32) * has_prev
    nx = next_ref[0].astype(F32) * has_next
    hp = pv.shape[0]
    r = lax.broadcasted_iota(I32, (n, 1), 0)
    xm1 = jnp.where(r == 0, pv[hp - 1:hp], pltpu.roll(xm, 1, axis=0))
    xm2 = jnp.where(r == 0, pv[hp - 2:hp - 1], jnp.where(r == 1, pv[hp - 1:hp], pltpu.roll(xm, 2, axis=0)))
    xp1 = jnp.where(r == n - 1, nx[0:1], pltpu.roll(xm, n - 1, axis=0))
    w = w_ref[...]
    y = w[0:1] * xm2 + w[1:2] * xm1 + w[2:3] * xm + w[3:4] * xp1 + b_ref[...]
    return y * jax.nn.sigmoid(y)


def _ssd_kernel(*refs, reverse, final):
    (x_ref, xp_ref, xn_ref, bc_ref, bcp_ref, bcn_ref, dt_ref, cwx_ref, cbx_ref, cwb_ref, cbb_ref,
     dtb_ref, alog_ref, e_ref) = refs[:14]
    if final:
        z_ref, yprev_ref, dskip_ref, nw_ref, o_ref, h_ref = refs[14:]
    else:
        o_ref, h_ref = refs[14:]
    step = pl.program_id(1)
    nc = pl.num_programs(1)
    c = (nc - 1 - step) if reverse else step
    ln = SSD_CHUNK
    doff = SSD_HEADS if reverse else 0
    gw = SSD_INNER // SSD_GROUPS

    @pl.when(step == 0)
    def _():
        h_ref[...] = jnp.zeros_like(h_ref)

    has_prev = (c > 0).astype(F32)
    has_next = (c < nc - 1).astype(F32)
    xs = _conv_silu(x_ref, xp_ref, xn_ref, cwx_ref, cbx_ref, has_prev, has_next)
    bc = _conv_silu(bc_ref, bcp_ref, bcn_ref, cwb_ref, cbb_ref, has_prev, has_next)

    dt = jax.nn.softplus(dt_ref[0] + dtb_ref[...])
    a_dt = dt * (-jnp.exp(alog_ref[...]))
    tr = lax.broadcasted_iota(I32, (ln, ln), 0)
    tc = lax.broadcasted_iota(I32, (ln, ln), 1)
    scanned = (tc >= tr) if reverse else (tc <= tr)
    tri = scanned.astype(BF16)
    a1, a2, a3 = _split3(a_dt)
    cum = _dot(tri, a1) + _dot(tri, a2) + _dot(tri, a3)
    tot = jnp.sum(a_dt, axis=0, keepdims=True)
    cum_t = cum.T

    stacked = jnp.concatenate([dt, jnp.exp(tot - cum), jnp.exp(cum), jnp.broadcast_to(jnp.exp(tot), (8, LANES))],
                              axis=0)
    s_hi, s_lo = _split2(stacked)
    expanded = _dot(s_hi, e_ref[...]) + _dot(s_lo, e_ref[...])
    dt_e, wst_e, ecum_e, etot_e = (expanded[:ln], expanded[ln:2 * ln], expanded[2 * ln:3 * ln],
                                   expanded[3 * ln:3 * ln + 1])
    xdt = xs * dt_e
    xw = (xdt * wst_e).astype(BF16)
    xdt_b = xdt.astype(BF16)

    bm = bc[:, :SSD_GROUPS * SSD_STATE].astype(BF16)
    cm = bc[:, SSD_GROUPS * SSD_STATE:].astype(BF16)
    y_parts = []
    new_states = []
    for g in range(SSD_GROUPS):
        b_g = bm[:, g * SSD_STATE:(g + 1) * SSD_STATE]
        c_g = cm[:, g * SSD_STATE:(g + 1) * SSD_STATE]
        cb = _dot_nt(c_g, b_g)
        for pair in range(2):
            ms = []
            for hh in range(2):
                lane = doff + 4 * g + 2 * pair + hh
                seg = cum[:, lane:lane + 1] - cum_t[lane:lane + 1, :]
                ms.append((cb * jnp.exp(jnp.where(scanned, seg, NEG_INF))).astype(BF16))
            k = 2 * g + pair
            y_parts.append(_dot(jnp.concatenate(ms, axis=1), _stack_halves(xdt_b[:, k * LANES:(k + 1) * LANES])))
        h_g = h_ref[:, g * gw:(g + 1) * gw]
        y_off = _dot(c_g, h_g.astype(BF16)) * ecum_e[:, g * gw:(g + 1) * gw]
        y_parts[2 * g] = y_parts[2 * g] + y_off[:, :LANES]
        y_parts[2 * g + 1] = y_parts[2 * g + 1] + y_off[:, LANES:]
        new_states.append(h_g * etot_e[:, g * gw:(g + 1) * gw] + _dot_tn(b_g, xw[:, g * gw:(g + 1) * gw]))
    y = jnp.concatenate(y_parts, axis=1)
    h_ref[...] = jnp.concatenate(new_states, axis=1)

    if final:
        y = y + yprev_ref[0].astype(F32) + dskip_ref[...] * xs
        z = z_ref[0].astype(F32)
        y = y * (z * jax.nn.sigmoid(z))
        outs = []
        for g in range(SSD_GROUPS):
            yg = y[:, g * gw:(g + 1) * gw]
            outs.append(yg * lax.rsqrt(jnp.mean(yg * yg, axis=-1, keepdims=True) + NORM_EPS))
        y = jnp.concatenate(outs, axis=1) * nw_ref[...]
    o_ref[0] = y.astype(o_ref.dtype)


def _ssd_pass(zxbc, dt, conv_w, conv_b, dtb_row, alog_row, e_mat, reverse, final_args=None):
    bsz, seq, _ = zxbc.shape
    ln = SSD_CHUNK
    nc = seq // ln
    halo = 16
    hb = ln // halo
    final = final_args is not None

    def cidx(s):
        return (nc - 1 - s) if reverse else s

    def main_spec(col):
        return pl.BlockSpec((1, ln, SSD_INNER), lambda b, s: (b, cidx(s), col))

    def prev_spec(col):
        return pl.BlockSpec((1, halo, SSD_INNER), lambda b, s: (b, jnp.maximum(cidx(s) * hb - 1, 0), col))

    def next_spec(col):
        return pl.BlockSpec((1, halo, SSD_INNER),
                            lambda b, s: (b, jnp.minimum(cidx(s) * hb + hb, seq // halo - 1), col))

    def const_spec(shape, col=0):
        return pl.BlockSpec(shape, lambda b, s: (0, col))

    in_specs = [main_spec(1), prev_spec(1), next_spec(1), main_spec(2), prev_spec(2), next_spec(2),
                pl.BlockSpec((1, ln, LANES), lambda b, s: (b, cidx(s), 0)),
                const_spec((SSD_CONV, SSD_INNER), 0), const_spec((1, SSD_INNER), 0),
                const_spec((SSD_CONV, SSD_INNER), 1), const_spec((1, SSD_INNER), 1),
                const_spec((1, LANES)), const_spec((1, LANES)), const_spec((LANES, SSD_INNER))]
    args = [zxbc, zxbc, zxbc, zxbc, zxbc, zxbc, dt, conv_w, conv_b, conv_w, conv_b, dtb_row, alog_row, e_mat]
    if final:
        yprev, dskip, nw = final_args
        in_specs += [main_spec(0), main_spec(0), const_spec((1, SSD_INNER)), const_spec((1, SSD_INNER))]
        args += [zxbc, yprev, dskip, nw]
    return pl.pallas_call(
        functools.partial(_ssd_kernel, reverse=reverse, final=final),
        grid=(bsz, nc),
        in_specs=in_specs,
        out_specs=main_spec(0),
        out_shape=jax.ShapeDtypeStruct((bsz, seq, SSD_INNER), BF16),
        scratch_shapes=[pltpu.VMEM((SSD_STATE, SSD_INNER), F32)],
        compiler_params=_cparams(("parallel", "arbitrary")),
        name="ssd_bwd_final" if reverse else "ssd_fwd",
    )(*args)


def ssd_expand_matrix(reverse):
    rows = jnp.arange(LANES)[:, None]
    head = jnp.arange(SSD_INNER)[None, :] // HEAD_DIM
    return (rows == head + (SSD_HEADS if reverse else 0)).astype(BF16)


def ssd_mixer(zxbc, dt, conv_w, conv_b, dtb_row, alog_row, dskip, nw):
    y_f = _ssd_pass(zxbc, dt, conv_w, conv_b, dtb_row, alog_row, ssd_expand_matrix(False), reverse=False)
    return _ssd_pass(zxbc, dt, conv_w, conv_b, dtb_row, alog_row, ssd_expand_matrix(True), reverse=True,
                     final_args=(y_f, dskip, nw))


def _merge_kernel(x_ref, nw_ref, wg_ref, ona_ref, oswa_ref, ossd_ref, wna_ref, wswa_ref, wssd_ref, wout_ref,
                  nffn_ref, rhi_ref, rlo_ref, xo_ref, aff_ref):
    x = x_ref[...]
    h = _rmsnorm(x, nw_ref[...]).astype(BF16)
    d = x.shape[1]
    gates = jax.nn.sigmoid(_dot(h, wg_ref[...]))
    merged = (gates[:, :d] * _dot(ona_ref[...], wna_ref[...])
              + gates[:, d:2 * d] * _dot(oswa_ref[...], wswa_ref[...])
              + gates[:, 2 * d:] * _dot(ossd_ref[...], wssd_ref[...]))
    xn = x + _dot(merged.astype(BF16), wout_ref[...])
    xo_ref[...] = xn
    h2 = _rmsnorm(xn, nffn_ref[...])
    h_hi, h_lo = _split2(h2)
    logits = _dot(h_hi, rhi_ref[...]) + _dot(h_hi, rlo_ref[...]) + _dot(h_lo, rhi_ref[...])
    live = lax.broadcasted_iota(I32, logits.shape, 1) < N_EXPERTS
    logits = jnp.where(live, logits, NEG_INF)
    e = jnp.exp(logits - jnp.max(logits, axis=-1, keepdims=True))
    aff_ref[...] = e / jnp.sum(e, axis=-1, keepdims=True)


def merge_and_route(x2d, nw, wg, o_na, o_swa, o_ssd, w_na, w_swa, w_ssd, w_out, nffn, r_hi, r_lo, tm=256):
    t, d = x2d.shape
    tm = min(tm, t)
    row = lambda i: (i, 0)
    fixed = lambda i: (0, 0)
    return pl.pallas_call(
        _merge_kernel,
        grid=(t // tm,),
        in_specs=[pl.BlockSpec((tm, d), row), pl.BlockSpec((1, d), fixed), pl.BlockSpec(wg.shape, fixed),
                  pl.BlockSpec((tm, d), row), pl.BlockSpec((tm, d), row), pl.BlockSpec((tm, d), row),
                  pl.BlockSpec((d, d), fixed), pl.BlockSpec((d, d), fixed), pl.BlockSpec((d, d), fixed),
                  pl.BlockSpec((d, d), fixed), pl.BlockSpec((1, d), fixed),
                  pl.BlockSpec((d, LANES), fixed), pl.BlockSpec((d, LANES), fixed)],
        out_specs=[pl.BlockSpec((tm, d), row), pl.BlockSpec((tm, LANES), row)],
        out_shape=[jax.ShapeDtypeStruct((t, d), F32), jax.ShapeDtypeStruct((t, LANES), F32)],
        compiler_params=_cparams(("parallel",)),
        name="merge_and_route",
    )(x2d, nw, wg, o_na, o_swa, o_ssd, w_na, w_swa, w_ssd, w_out, nffn, r_hi, r_lo)


def _prefix_rows(mask_f, upper, lower_strict):
    lp = _dot(mask_f.astype(BF16), upper)
    rc = jnp.broadcast_to(lp[:, LANES - 1:LANES], lp.shape)
    return lp, rc, _dot(lower_strict, rc.astype(BF16))


def _select_kernel(a_ref, idx_ref, wts_ref, slot_ref, rstart_ref, *, cap, chunk):
    a = a_ref[0]
    nr = a.shape[0]
    bits = pltpu.bitcast(a, I32)

    def search(i, v):
        cand = v | lax.shift_left(jnp.int32(1), 30 - i)
        cnt = jnp.sum((bits >= cand).astype(F32))
        return jnp.where(cnt >= cap, cand, v)

    thr = lax.fori_loop(0, 31, search, jnp.int32(0))
    gt = bits > thr
    eq = bits == thr
    need = cap - jnp.sum(gt.astype(F32))

    li = lax.broadcasted_iota(I32, (LANES, LANES), 0)
    lj = lax.broadcasted_iota(I32, (LANES, LANES), 1)
    upper = (li <= lj).astype(BF16)
    ri = lax.broadcasted_iota(I32, (nr, nr), 0)
    rj = lax.broadcasted_iota(I32, (nr, nr), 1)
    lower_strict = (rj < ri).astype(BF16)

    elp, _, ers = _prefix_rows(eq.astype(F32), upper, lower_strict)
    sel = gt | (eq & (elp + ers <= need))
    lp, rc, rs = _prefix_rows(sel.astype(F32), upper, lower_strict)
    slot_ref[0] = jnp.where(sel, lp + rs - 1.0, -1.0).astype(I32)
    rstart_ref[0] = rs.astype(I32)

    row_end = rs[:, 0:1] + rc[:, 0:1]
    rcount = rc[:, 0:1]
    lp_b = lp.astype(BF16)
    a1, a2, a3 = _split3(a)
    sub_r = lax.broadcasted_iota(I32, (nr, chunk), 0).astype(F32)
    sub_l = lax.broadcasted_iota(I32, (LANES, chunk), 0).astype(F32)
    for ch in range(cap // chunk):
        s = (ch * chunk + lax.broadcasted_iota(I32, (1, chunk), 1)).astype(F32)
        before = (row_end <= s).astype(F32)
        row_of = jnp.sum(before, axis=0, keepdims=True)
        j = s - jnp.sum(before * rcount, axis=0, keepdims=True)
        onehot = (sub_r == row_of).astype(BF16)
        g = _dot_tn(lp_b, onehot)
        lane_of = jnp.sum((g <= j).astype(F32), axis=0, keepdims=True)
        idx_ref[0, :, ch * chunk:(ch + 1) * chunk] = (row_of * LANES + lane_of).astype(I32)
        ga = _dot_tn(a1, onehot) + _dot_tn(a2, onehot) + _dot_tn(a3, onehot)
        wts_ref[0, :, ch * chunk:(ch + 1) * chunk] = jnp.sum(jnp.where(sub_l == lane_of, ga, 0.0), axis=0,
                                                               keepdims=True)


def select_tokens(aff_t, cap):
    ne, nr, _ = aff_t.shape
    chunk = min(1024, cap)
    blk = lambda e: (e, 0, 0)
    return pl.pallas_call(
        functools.partial(_select_kernel, cap=cap, chunk=chunk),
        grid=(ne,),
        in_specs=[pl.BlockSpec((1, nr, LANES), blk)],
        out_specs=[pl.BlockSpec((1, 1, cap), blk), pl.BlockSpec((1, 1, cap), blk),
                   pl.BlockSpec((1, nr, LANES), blk), pl.BlockSpec((1, nr, LANES), blk)],
        out_shape=[jax.ShapeDtypeStruct((ne, 1, cap), I32), jax.ShapeDtypeStruct((ne, 1, cap), F32),
                   jax.ShapeDtypeStruct((ne, nr, LANES), I32), jax.ShapeDtypeStruct((ne, nr, LANES), I32)],
        compiler_params=_cparams(("parallel",)),
        name="select_tokens",
    )(aff_t)


def _expert_kernel(idx_hbm, x_hbm, w_ref, nffn_ref, wg_ref, wu_ref, wd_ref, o_ref, idx_smem, xbuf, isem, gsem,
                   *, tm):
    nt = pl.num_programs(1)
    n = pl.program_id(0) * nt + pl.program_id(1)
    total = pl.num_programs(0) * nt
    slot = n % 2

    def idx_copy(step, s):
        return pltpu.make_async_copy(idx_hbm.at[step], idx_smem.at[s], isem.at[s])

    def gather_wait(s):
        pltpu.make_async_copy(x_hbm.at[pl.ds(0, tm), :], xbuf.at[s], gsem.at[s]).wait()

    def issue_gather(s):
        def body(r, carry):
            tok = idx_smem[s, r]
            pltpu.make_async_copy(x_hbm.at[pl.ds(tok, 1), :], xbuf.at[s, pl.ds(r, 1), :], gsem.at[s]).start()
            return carry
        lax.fori_loop(0, tm, body, 0)

    @pl.when(n == 0)
    def _():
        idx_copy(0, 0).start()
        idx_copy(0, 0).wait()
        issue_gather(0)

        @pl.when(total > 1)
        def _():
            idx_copy(1, 1).start()

    @pl.when(n + 1 < total)
    def _():
        idx_copy(n + 1, 1 - slot).wait()
        issue_gather(1 - slot)

    @pl.when(n + 2 < total)
    def _():
        idx_copy(n + 2, slot).start()

    gather_wait(slot)
    h = _rmsnorm(xbuf[slot], nffn_ref[...]).astype(BF16)
    gate = _dot(h, wg_ref[0])
    up = _dot(h, wu_ref[0])
    act = (gate * jax.nn.sigmoid(gate) * up).astype(BF16)
    y = _dot(act, wd_ref[0])
    eye = lax.broadcasted_iota(I32, (LANES, LANES), 0) == lax.broadcasted_iota(I32, (LANES, LANES), 1)
    wcol = jnp.concatenate(
        [jnp.sum(jnp.where(eye, jnp.broadcast_to(w_ref[0, r:r + 1, :], (LANES, LANES)), 0.0), axis=1, keepdims=True)
         for r in range(tm // LANES)], axis=0)
    o_ref[...] = (y * wcol).astype(o_ref.dtype)


def run_experts(idx_tiles, x2d, wts_rows, nffn, wg, wu, wd, cap, tm):
    ne = wg.shape[0]
    t, d = x2d.shape
    nt = cap // tm
    ff = wg.shape[2]
    return pl.pallas_call(
        functools.partial(_expert_kernel, tm=tm),
        grid=(ne, nt),
        in_specs=[pl.BlockSpec(memory_space=pl.ANY), pl.BlockSpec(memory_space=pl.ANY),
                  pl.BlockSpec((1, tm // LANES, LANES), lambda e, i: (e * nt + i, 0, 0)),
                  pl.BlockSpec((1, d), lambda e, i: (0, 0)),
                  pl.BlockSpec((1, d, ff), lambda e, i: (e, 0, 0)),
                  pl.BlockSpec((1, d, ff), lambda e, i: (e, 0, 0)),
                  pl.BlockSpec((1, ff, d), lambda e, i: (e, 0, 0))],
        out_specs=pl.BlockSpec((tm, d), lambda e, i: (e * nt + i, 0)),
        out_shape=jax.ShapeDtypeStruct((ne * cap, d), BF16),
        scratch_shapes=[pltpu.SMEM((2, tm), I32), pltpu.VMEM((2, tm, d), F32),
                        pltpu.SemaphoreType.DMA((2,)), pltpu.SemaphoreType.DMA((2,))],
        compiler_params=_cparams(("arbitrary", "arbitrary")),
        name="run_experts",
    )(idx_tiles, x2d, wts_rows, nffn, wg, wu, wd)


def _combine_kernel(rstart_ref, x_ref, slot_ref, ye_hbm, o_ref, ybuf, sem, *, cap, win, align):
    r = pl.program_id(0)
    nr = pl.num_programs(0)
    ne = N_EXPERTS
    cur = r % 2
    last = ne * cap - win

    def start_of(row, e):
        p = e * cap + rstart_ref[row * ne + e]
        return jnp.minimum((p // align) * align, last)

    def window_copy(row, e, s):
        st = pl.multiple_of(start_of(row, e), align)
        return pltpu.make_async_copy(ye_hbm.at[pl.ds(st, win), :], ybuf.at[s, e], sem.at[s, e])

    @pl.when(r == 0)
    def _():
        for e in range(ne):
            window_copy(0, e, 0).start()

    @pl.when(r + 1 < nr)
    def _():
        for e in range(ne):
            window_copy(r + 1, e, 1 - cur).start()

    acc = x_ref[...]
    slots = slot_ref[...]
    lane = lax.broadcasted_iota(I32, (LANES, win), 1)
    for e in range(ne):
        window_copy(r, e, cur).wait()
        sl = slots[:, e:e + 1]
        rel = jnp.where(sl >= 0, sl + (e * cap - start_of(r, e)), -1)
        acc = acc + _dot((lane == rel).astype(BF16), ybuf[cur, e])
    o_ref[...] = acc


def combine(x2d, slot_t, rstart_flat, ye, cap):
    t, d = x2d.shape
    nr = t // LANES
    align = 16
    win = LANES + align
    grid_spec = pltpu.PrefetchScalarGridSpec(
        num_scalar_prefetch=1,
        grid=(nr,),
        in_specs=[pl.BlockSpec((LANES, d), lambda r, rs: (r, 0)),
                  pl.BlockSpec((LANES, N_EXPERTS), lambda r, rs: (r, 0)),
                  pl.BlockSpec(memory_space=pl.ANY)],
        out_specs=pl.BlockSpec((LANES, d), lambda r, rs: (r, 0)),
        scratch_shapes=[pltpu.VMEM((2, N_EXPERTS, win, d), BF16), pltpu.SemaphoreType.DMA((2, N_EXPERTS))],
    )
    return pl.pallas_call(
        functools.partial(_combine_kernel, cap=cap, win=win, align=align),
        grid_spec=grid_spec,
        out_shape=jax.ShapeDtypeStruct((t, d), F32),
        compiler_params=_cparams(("arbitrary",)),
        name="combine",
    )(rstart_flat, x2d, slot_t, ye)


def moe_layer(x2d, aff, nffn, wg, wu, wd):
    t, d = x2d.shape
    ne = N_EXPERTS
    cap = CAPACITY_FACTOR * t // ne
    nr = t // LANES
    tm = min(512, cap)
    aff_t = aff[:, :ne].T.reshape(ne, nr, LANES)
    idx, wts, slot, rstart = select_tokens(aff_t, cap)
    ye = run_experts(idx.reshape(ne * cap // tm, tm), x2d, wts.reshape(ne * cap // tm, tm // LANES, LANES), nffn,
                     wg, wu, wd, cap, tm)
    slot_t = slot.reshape(ne, t).T
    rstart_flat = rstart[:, :, 0].T.reshape(nr * ne)
    return combine(x2d, slot_t, rstart_flat, ye, cap)


def _prep_layer(p, l):
    w_in = p["w_in"][l]
    d = D_MODEL
    c_na, c_swa = 3 * d, d + 2 * SWA_KV_HEADS * HEAD_DIM
    o_ssd = c_na + c_swa
    c_ssd = d + (d + 2 * SSD_GROUPS * SSD_STATE)
    o_dt = o_ssd + c_ssd
    o_gate = o_dt + 2 * SSD_HEADS
    w_dt = jnp.pad(w_in[:, o_dt:o_gate], ((0, 0), (0, LANES - 2 * SSD_HEADS)))
    tile2 = lambda v: jnp.tile(v.astype(F32), 2)[None, :]
    pad_row = lambda v: jnp.pad(v.astype(F32).reshape(-1), (0, LANES - 2 * SSD_HEADS))[None, :]
    r_hi, r_lo = _split2(jnp.pad(p["router_w"][l], ((0, 0), (0, LANES - N_EXPERTS))))
    return dict(
        norm_mix=p["norm_mix"][l][None, :],
        w_na_in=w_in[:, :c_na].astype(BF16),
        w_swa_in=w_in[:, c_na:o_ssd].astype(BF16),
        w_ssd_in=w_in[:, o_ssd:o_dt].astype(BF16),
        w_dt=w_dt.astype(BF16),
        w_gate=w_in[:, o_gate:].astype(BF16),
        na_qn=tile2(p["na_q_norm"][l]), na_kn=tile2(p["na_k_norm"][l]),
        na_tab=na_bias_table(p["na_rel_bias"][l]),
        swa_qn=tile2(p["swa_q_norm"][l]), swa_kn=tile2(p["swa_k_norm"][l]),
        swa_sink=p["swa_sink"][l].astype(F32),
        conv_w=p["ssd_conv_w"][l].astype(F32), conv_b=p["ssd_conv_b"][l].astype(F32)[None, :],
        dtb_row=pad_row(p["ssd_dt_bias"][l]), alog_row=pad_row(p["ssd_a_log"][l]),
        dskip=jnp.repeat(p["ssd_d"][l].astype(F32), HEAD_DIM)[None, :],
        ssd_norm=p["ssd_norm"][l].astype(F32)[None, :],
        w_na=p["w_branch_na"][l].astype(BF16), w_swa=p["w_branch_swa"][l].astype(BF16),
        w_ssd=p["w_branch_ssd"][l].astype(BF16), w_out=p["w_out"][l].astype(BF16),
        norm_ffn=p["norm_ffn"][l][None, :], r_hi=r_hi, r_lo=r_lo,
        wg=p["expert_w_gate"][l].astype(BF16), wu=p["expert_w_up"][l].astype(BF16),
        wd=p["expert_w_down"][l].astype(BF16),
    )


def _layer(x, lp, cos, sin):
    bsz, seq, d = x.shape
    t = bsz * seq
    x2d = x.reshape(t, d)
    (qkv_na,) = norm_proj(x2d, lp["norm_mix"], [lp["w_na_in"]], [BF16])
    (qkv_swa,) = norm_proj(x2d, lp["norm_mix"], [lp["w_swa_in"]], [BF16])
    zxbc, dt = norm_proj(x2d, lp["norm_mix"], [lp["w_ssd_in"], lp["w_dt"]], [BF16, F32])
    o_na = neighborhood_attention(qkv_na.reshape(bsz, seq, -1), lp["na_qn"], lp["na_kn"], lp["na_tab"])
    o_swa = sliding_window_attention(qkv_swa.reshape(bsz, seq, -1), lp["swa_sink"], lp["swa_qn"], lp["swa_kn"],
                                     cos, sin)
    o_ssd = ssd_mixer(zxbc.reshape(bsz, seq, -1), dt.reshape(bsz, seq, -1), lp["conv_w"], lp["conv_b"],
                      lp["dtb_row"], lp["alog_row"], lp["dskip"], lp["ssd_norm"])
    x_mid, aff = merge_and_route(x2d, lp["norm_mix"], lp["w_gate"], o_na.reshape(t, d), o_swa.reshape(t, d),
                                 o_ssd.reshape(t, d), lp["w_na"], lp["w_swa"], lp["w_ssd"], lp["w_out"],
                                 lp["norm_ffn"], lp["r_hi"], lp["r_lo"])
    return moe_layer(x_mid, aff, lp["norm_ffn"], lp["wg"], lp["wu"], lp["wd"]).reshape(bsz, seq, d)


def kernel(x_prompt, x_sample, norm_mix, w_in, na_q_norm, na_k_norm, na_rel_bias, swa_q_norm, swa_k_norm, swa_sink, ssd_conv_w, ssd_conv_b, ssd_dt_bias, ssd_a_log, ssd_d, ssd_norm, w_branch_na, w_branch_swa, w_branch_ssd, w_out, norm_ffn, router_w, expert_w_gate, expert_w_up, expert_w_down):
    p = dict(norm_mix=norm_mix, w_in=w_in, na_q_norm=na_q_norm, na_k_norm=na_k_norm, na_rel_bias=na_rel_bias,
             swa_q_norm=swa_q_norm, swa_k_norm=swa_k_norm, swa_sink=swa_sink, ssd_conv_w=ssd_conv_w,
             ssd_conv_b=ssd_conv_b, ssd_dt_bias=ssd_dt_bias, ssd_a_log=ssd_a_log, ssd_d=ssd_d, ssd_norm=ssd_norm,
             w_branch_na=w_branch_na, w_branch_swa=w_branch_swa, w_branch_ssd=w_branch_ssd, w_out=w_out,
             norm_ffn=norm_ffn, router_w=router_w, expert_w_gate=expert_w_gate, expert_w_up=expert_w_up,
             expert_w_down=expert_w_down)
    depth = w_in.shape[0]
    layers = [_prep_layer(p, l) for l in range(depth)]
    outs = []
    for x in (x_prompt, x_sample):
        cos, sin = rope_tables(x.shape[1])
        for lp in layers:
            x = _layer(x, lp, cos, sin)
        outs.append(x)
    return tuple(outs)
```

```python
import functools
import math

import jax
import jax.numpy as jnp
from jax import lax
from jax.experimental import pallas as pl
from jax.experimental.pallas import tpu as pltpu

F32 = jnp.float32
BF16 = jnp.bfloat16
I32 = jnp.int32

D_MODEL = 1024
HEAD_DIM = 64
GRID_W = 64
NA_HEADS = 16
NA_WIN_ROWS = 8
NA_WIN_COLS = 16
NA_QROWS = 4
NA_KROWS = 12
SWA_Q_HEADS = 16
SWA_KV_HEADS = 4
SWA_WINDOW = 128
SWA_BLOCK = 128
ROPE_THETA = 10000.0
SSD_HEADS = 16
SSD_GROUPS = 4
SSD_STATE = 128
SSD_CONV = 4
SSD_CHUNK = 128
SSD_INNER = 1024
N_EXPERTS = 16
EXPERT_FF = 2048
CAPACITY_FACTOR = 2
NORM_EPS = 1e-6
NEG_INF = -1e30
LANES = 128
VMEM_LIMIT = 56 * 1024 * 1024


def _cparams(sem):
    return pltpu.CompilerParams(dimension_semantics=sem, vmem_limit_bytes=VMEM_LIMIT)


def _rmsnorm(x, w):
    return x * lax.rsqrt(jnp.mean(x * x, axis=-1, keepdims=True) + NORM_EPS) * w


def _lo_mask(shape):
    return lax.broadcasted_iota(I32, shape, len(shape) - 1) < HEAD_DIM


def _headnorm2(x, w):
    lo = _lo_mask(x.shape)
    x2 = x * x
    sa = jnp.sum(jnp.where(lo, x2, 0.0), axis=-1, keepdims=True)
    sb = jnp.sum(jnp.where(lo, 0.0, x2), axis=-1, keepdims=True)
    ms = jnp.where(lo, sa, sb) * (1.0 / HEAD_DIM)
    return x * lax.rsqrt(ms + NORM_EPS) * w


def _stack_halves(x):
    lo = _lo_mask(x.shape)
    zero = jnp.zeros_like(x)
    return jnp.concatenate([jnp.where(lo, x, zero), jnp.where(lo, zero, x)], axis=0)


def _dot(a, b):
    return jnp.dot(a, b, preferred_element_type=F32)


def _dot_nt(a, b):
    return lax.dot_general(a, b, (((1,), (1,)), ((), ())), preferred_element_type=F32)


def _dot_tn(a, b):
    return lax.dot_general(a, b, (((0,), (0,)), ((), ())), preferred_element_type=F32)


def _split2(x):
    hi = x.astype(BF16)
    lo = (x - hi.astype(F32)).astype(BF16)
    return hi, lo


def _split3(x):
    a = x.astype(BF16)
    r = x - a.astype(F32)
    b = r.astype(BF16)
    c = (r - b.astype(F32)).astype(BF16)
    return a, b, c


def _norm_proj_kernel(*refs, n_out):
    x_ref, nw_ref = refs[0], refs[1]
    w_refs = refs[2:2 + n_out]
    o_refs = refs[2 + n_out:2 + 2 * n_out]
    h = _rmsnorm(x_ref[...], nw_ref[...]).astype(BF16)
    for w_ref, o_ref in zip(w_refs, o_refs):
        o_ref[...] = _dot(h, w_ref[...]).astype(o_ref.dtype)


def norm_proj(x2d, nw, weights, out_dtypes, tm=512):
    t, d = x2d.shape
    tm = min(tm, t)
    n_out = len(weights)
    return pl.pallas_call(
        functools.partial(_norm_proj_kernel, n_out=n_out),
        grid=(t // tm,),
        in_specs=[pl.BlockSpec((tm, d), lambda i: (i, 0)), pl.BlockSpec((1, d), lambda i: (0, 0))]
        + [pl.BlockSpec(w.shape, lambda i: (0, 0)) for w in weights],
        out_specs=[pl.BlockSpec((tm, w.shape[1]), lambda i: (i, 0)) for w in weights],
        out_shape=[jax.ShapeDtypeStruct((t, w.shape[1]), dt) for w, dt in zip(weights, out_dtypes)],
        compiler_params=_cparams(("parallel",)),
        name="norm_proj",
    )(x2d, nw, *weights)


def _rope(x, cos, sin_signed):
    first = (lax.broadcasted_iota(I32, x.shape, 1) % HEAD_DIM) < HEAD_DIM // 2
    partner = jnp.where(first, pltpu.roll(x, LANES - HEAD_DIM // 2, axis=1), pltpu.roll(x, HEAD_DIM // 2, axis=1))
    return x * cos + partner * sin_signed


PROJ_CHUNK = 512


def _norm_proj_heads_kernel(*refs, n_norm, rope):
    if rope:
        x_ref, nw_ref, w_ref, hw_ref, cos_ref, sin_ref, o_ref = refs
    else:
        x_ref, nw_ref, w_ref, hw_ref, o_ref = refs
    h = _rmsnorm(x_ref[...], nw_ref[...]).astype(BF16)
    ncols = w_ref.shape[1]
    for c0 in range(0, ncols, PROJ_CHUNK):
        acc = _dot(h, w_ref[:, c0:c0 + PROJ_CHUNK])
        parts = []
        for j in range(PROJ_CHUNK // LANES):
            col = c0 // LANES + j
            blk = acc[:, j * LANES:(j + 1) * LANES]
            if col < n_norm:
                blk = _headnorm2(blk, hw_ref[:, col * LANES:(col + 1) * LANES])
                if rope:
                    blk = _rope(blk, cos_ref[...], sin_ref[...])
            parts.append(blk)
        o_ref[:, c0:c0 + PROJ_CHUNK] = jnp.concatenate(parts, axis=1).astype(o_ref.dtype)


def norm_proj_heads(x2d, nw, w, head_w, seq, cos=None, sin=None, tm=512):
    t, d = x2d.shape
    tm = min(tm, t, seq)
    n = w.shape[1]
    n_norm = head_w.shape[1] // LANES
    rope = cos is not None
    in_specs = [pl.BlockSpec((tm, d), lambda i: (i, 0)), pl.BlockSpec((1, d), lambda i: (0, 0)),
                pl.BlockSpec(w.shape, lambda i: (0, 0)), pl.BlockSpec(head_w.shape, lambda i: (0, 0))]
    args = [x2d, nw, w, head_w]
    if rope:
        per_seq = seq // tm
        in_specs += [pl.BlockSpec((tm, LANES), lambda i: (i % per_seq, 0))] * 2
        args += [cos, sin]
    return pl.pallas_call(
        functools.partial(_norm_proj_heads_kernel, n_norm=n_norm, rope=rope),
        grid=(t // tm,),
        in_specs=in_specs,
        out_specs=pl.BlockSpec((tm, n), lambda i: (i, 0)),
        out_shape=jax.ShapeDtypeStruct((t, n), BF16),
        compiler_params=_cparams(("parallel",)),
        name="norm_proj_heads",
    )(*args)


NA_PAIRS = 2


def _na_kernel(q_ref, k0_ref, k1_ref, k2_ref, v0_ref, v1_ref, v2_ref, tab_ref, o_ref):
    g = pl.program_id(2)
    ng = pl.num_programs(2)
    variant = jnp.where(g == 0, 0, jnp.where(g == ng - 1, 2, 1))
    nk = NA_KROWS * GRID_W
    outs = []
    for hp in range(NA_PAIRS):
        cols = slice(hp * LANES, (hp + 1) * LANES)
        kw = jnp.concatenate([k0_ref[0, :, cols], k1_ref[0, :, cols], k2_ref[0, :, cols]], axis=0)
        s = _dot_nt(q_ref[0, :, cols], _stack_halves(kw)) + tab_ref[hp, variant]
        sa, sb = s[:, :nk], s[:, nk:]
        pa = jnp.exp(sa - jnp.max(sa, axis=-1, keepdims=True))
        pb = jnp.exp(sb - jnp.max(sb, axis=-1, keepdims=True))
        la = jnp.sum(pa, axis=-1, keepdims=True)
        lb = jnp.sum(pb, axis=-1, keepdims=True)
        p = jnp.concatenate([pa, pb], axis=1).astype(BF16)
        vw = jnp.concatenate([v0_ref[0, :, cols], v1_ref[0, :, cols], v2_ref[0, :, cols]], axis=0)
        o = _dot(p, _stack_halves(vw))
        outs.append(o * jnp.where(_lo_mask(o.shape), 1.0 / la, 1.0 / lb))
    o_ref[0] = jnp.concatenate(outs, axis=1).astype(o_ref.dtype)


def na_bias_table(rel_bias):
    nh = rel_bias.shape[0]
    offs = jnp.array([0, NA_QROWS, 2 * NA_QROWS], I32)
    r0w = jnp.array([[0, 0, 0, 0], [0, 1, 2, 3], [4, 4, 4, 4]], I32)
    j = jnp.arange(NA_QROWS, dtype=I32)
    i = jnp.arange(NA_KROWS, dtype=I32)
    c = jnp.arange(GRID_W, dtype=I32)
    qrow = offs[:, None] + j[None, :]
    row_rel = i[None, None, :] - qrow[:, :, None] + NA_WIN_ROWS - 1
    row_ok = (i[None, None, :] >= r0w[:, :, None]) & (i[None, None, :] < r0w[:, :, None] + NA_WIN_ROWS)
    c0 = jnp.clip(c - NA_WIN_COLS // 2, 0, GRID_W - NA_WIN_COLS)
    col_rel = c[None, :] - c[:, None] + NA_WIN_COLS - 1
    col_ok = (c[None, :] >= c0[:, None]) & (c[None, :] < c0[:, None] + NA_WIN_COLS)
    row_sel = (row_rel[..., None] == jnp.arange(2 * NA_WIN_ROWS - 1)).astype(F32)
    col_sel = (col_rel[..., None] == jnp.arange(2 * NA_WIN_COLS - 1)).astype(F32)
    by_col = jnp.einsum('hab,ckb->hack', rel_bias.astype(F32), col_sel, precision=lax.Precision.HIGHEST)
    b = jnp.einsum('vjia,hack->hvjcik', row_sel, by_col, precision=lax.Precision.HIGHEST)
    ok = row_ok[:, :, None, :, None] & col_ok[None, None, :, None, :]
    b = jnp.where(ok[None], b, NEG_INF)
    b = b.reshape(nh // 2, 2, 3, NA_QROWS * GRID_W, NA_KROWS * GRID_W)
    return jnp.concatenate([b[:, 0], b[:, 1]], axis=-1)


def neighborhood_attention(qkv, tab):
    bsz, seq, _ = qkv.shape
    rows = seq // GRID_W
    assert rows % NA_QROWS == 0 and rows >= NA_KROWS + NA_QROWS
    ng = rows // NA_QROWS
    tq = NA_QROWS * GRID_W
    wblk = NA_PAIRS * LANES
    nblk = NA_HEADS * HEAD_DIM // wblk

    def kv_spec(col0, j):
        return pl.BlockSpec((1, tq, wblk), lambda hp, b, g: (b, jnp.clip(g - 1, 0, ng - 3) + j, col0 + hp))

    return pl.pallas_call(
        _na_kernel,
        grid=(nblk, bsz, ng),
        in_specs=[pl.BlockSpec((1, tq, wblk), lambda hp, b, g: (b, g, hp))]
        + [kv_spec(nblk, j) for j in range(3)] + [kv_spec(2 * nblk, j) for j in range(3)]
        + [pl.BlockSpec((NA_PAIRS, 3, tq, 2 * NA_KROWS * GRID_W), lambda hp, b, g: (hp, 0, 0, 0))],
        out_specs=pl.BlockSpec((1, tq, wblk), lambda hp, b, g: (b, g, hp)),
        out_shape=jax.ShapeDtypeStruct((bsz, seq, NA_HEADS * HEAD_DIM), BF16),
        compiler_params=_cparams(("parallel", "parallel", "parallel")),
        name="neighborhood_attention",
    )(qkv, qkv, qkv, qkv, qkv, qkv, qkv, tab)


def _swa_kernel(sink_ref, q_ref, kp_ref, kc_ref, kx_ref, vp_ref, vc_ref, vx_ref, o_ref, *, seq):
    i = pl.program_id(1)
    blk = SWA_BLOCK
    span = 3 * blk
    qpos = i * blk + lax.broadcasted_iota(I32, (2 * blk, span), 0) % blk
    kpos = (i - 1) * blk + lax.broadcasted_iota(I32, (2 * blk, span), 1)
    valid = (jnp.abs(kpos - qpos) <= SWA_WINDOW) & (kpos >= 0) & (kpos < seq)
    upper = lax.broadcasted_iota(I32, (2 * blk, 1), 0) < blk
    group = SWA_Q_HEADS // SWA_KV_HEADS
    qw = group * HEAD_DIM

    def softmax_part(sc, sink):
        sc = jnp.where(valid, sc, NEG_INF)
        m = jnp.maximum(jnp.max(sc, axis=-1, keepdims=True), sink)
        e = jnp.exp(sc - m)
        return e, jnp.sum(e, axis=-1, keepdims=True) + jnp.exp(sink - m)

    outs = []
    for kvh in range(SWA_KV_HEADS):
        q = q_ref[0, :, kvh * qw:(kvh + 1) * qw]
        q2 = jnp.concatenate([q[:, :LANES], q[:, LANES:]], axis=0)
        cols = slice(kvh * LANES, (kvh + 1) * LANES)
        kk = jnp.concatenate([kp_ref[0, :, cols], kc_ref[0, :, cols], kx_ref[0, :, cols]], axis=0)
        s = _dot_nt(q2, _stack_halves(kk))
        h0 = group * kvh
        ea, la = softmax_part(s[:, :span], jnp.where(upper, sink_ref[h0], sink_ref[h0 + 2]))
        eb, lb = softmax_part(s[:, span:], jnp.where(upper, sink_ref[h0 + 1], sink_ref[h0 + 3]))
        p = jnp.concatenate([ea, eb], axis=1).astype(BF16)
        vv = jnp.concatenate([vp_ref[0, :, cols], vc_ref[0, :, cols], vx_ref[0, :, cols]], axis=0)
        o = _dot(p, _stack_halves(vv))
        o = o * jnp.where(_lo_mask(o.shape), 1.0 / la, 1.0 / lb)
        outs.append(jnp.concatenate([o[:blk], o[blk:]], axis=1))
    o_ref[0] = jnp.concatenate(outs, axis=1).astype(o_ref.dtype)


def rope_tables(seq):
    half = HEAD_DIM // 2
    inv = ROPE_THETA ** (-jnp.arange(half, dtype=F32) / half)
    ang = jnp.arange(seq, dtype=F32)[:, None] * inv[None, :]
    cos, sin = jnp.cos(ang), jnp.sin(ang)
    return jnp.tile(cos, (1, 4)), jnp.tile(jnp.concatenate([-sin, sin], axis=1), (1, 2))


def sliding_window_attention(qkv, sink):
    bsz, seq, _ = qkv.shape
    blk = SWA_BLOCK
    nblk = seq // blk
    qcols = SWA_Q_HEADS * HEAD_DIM
    kvw = SWA_KV_HEADS * LANES

    def kv_spec(col, j):
        return pl.BlockSpec((1, blk, kvw), lambda b, i: (b, jnp.clip(i + j - 1, 0, nblk - 1), col))

    return pl.pallas_call(
        functools.partial(_swa_kernel, seq=seq),
        grid=(bsz, nblk),
        in_specs=[pl.BlockSpec(memory_space=pltpu.SMEM), pl.BlockSpec((1, blk, qcols), lambda b, i: (b, i, 0))]
        + [kv_spec(qcols // kvw, j) for j in range(3)] + [kv_spec(qcols // kvw + 1, j) for j in range(3)],
        out_specs=pl.BlockSpec((1, blk, qcols), lambda b, i: (b, i, 0)),
        out_shape=jax.ShapeDtypeStruct((bsz, seq, qcols), BF16),
        compiler_params=_cparams(("parallel", "parallel")),
        name="sliding_window_attention",
    )(sink, qkv, qkv, qkv, qkv, qkv, qkv, qkv)


def _conv_silu(main_ref, prev_ref, next_ref, w_ref, b_ref, has_prev, has_next):
    xm = main_ref[0].astype(F32)
    n = xm.shape[0]
    pv = prev_ref[0].astype(F32) * has_prev
    nx = next_ref[0].astype(F32) * has_next
    hp = pv.shape[0]
    r = lax.broadcasted_iota(I32, (n, 1), 0)
    xm1 = jnp.where(r == 0, pv[hp - 1:hp], pltpu.roll(xm, 1, axis=0))
    xm2 = jnp.where(r == 0, pv[hp - 2:hp - 1], jnp.where(r == 1, pv[hp - 1:hp], pltpu.roll(xm, 2, axis=0)))
    xp1 = jnp.where(r == n - 1, nx[0:1], pltpu.roll(xm, n - 1, axis=0))
    w = w_ref[...]
    y = w[0:1] * xm2 + w[1:2] * xm1 + w[2:3] * xm + w[3:4] * xp1 + b_ref[...]
    return y * jax.nn.sigmoid(y)


def _ssd_kernel(*refs, reverse, final):
    (x_ref, xp_ref, xn_ref, bc_ref, bcp_ref, bcn_ref, dt_ref, cwx_ref, cbx_ref, cwb_ref, cbb_ref,
     dtb_ref, alog_ref, e_ref) = refs[:14]
    if final:
        z_ref, yprev_ref, dskip_ref, nw_ref, o_ref, h_ref = refs[14:]
    else:
        o_ref, h_ref = refs[14:]
    step = pl.program_id(1)
    nc = pl.num_programs(1)
    c = (nc - 1 - step) if reverse else step
    ln = SSD_CHUNK
    doff = SSD_HEADS if reverse else 0
    gw = SSD_INNER // SSD_GROUPS

    @pl.when(step == 0)
    def _():
        h_ref[...] = jnp.zeros_like(h_ref)

    has_prev = (c > 0).astype(F32)
    has_next = (c < nc - 1).astype(F32)
    xs = _conv_silu(x_ref, xp_ref, xn_ref, cwx_ref, cbx_ref, has_prev, has_next)
    bc = _conv_silu(bc_ref, bcp_ref, bcn_ref, cwb_ref, cbb_ref, has_prev, has_next)

    dt = jax.nn.softplus(dt_ref[0] + dtb_ref[...])
    a_dt = dt * (-jnp.exp(alog_ref[...]))
    tr = lax.broadcasted_iota(I32, (ln, ln), 0)
    tc = lax.broadcasted_iota(I32, (ln, ln), 1)
    scanned = (tc >= tr) if reverse else (tc <= tr)
    tri = scanned.astype(BF16)
    a1, a2, a3 = _split3(a_dt)
    cum = _dot(tri, a1) + _dot(tri, a2) + _dot(tri, a3)
    tot = jnp.sum(a_dt, axis=0, keepdims=True)
    cum_t = cum.T

    stacked = jnp.concatenate([dt, jnp.exp(tot - cum), jnp.exp(cum), jnp.broadcast_to(jnp.exp(tot), (8, LANES))],
                              axis=0)
    s_hi, s_lo = _split2(stacked)
    expanded = _dot(s_hi, e_ref[...]) + _dot(s_lo, e_ref[...])
    dt_e, wst_e, ecum_e, etot_e = (expanded[:ln], expanded[ln:2 * ln], expanded[2 * ln:3 * ln],
                                   expanded[3 * ln:3 * ln + 1])
    xdt = xs * dt_e
    xw = (xdt * wst_e).astype(BF16)
    xdt_b = xdt.astype(BF16)

    bm = bc[:, :SSD_GROUPS * SSD_STATE].astype(BF16)
    cm = bc[:, SSD_GROUPS * SSD_STATE:].astype(BF16)
    y_parts = []
    new_states = []
    for g in range(SSD_GROUPS):
        b_g = bm[:, g * SSD_STATE:(g + 1) * SSD_STATE]
        c_g = cm[:, g * SSD_STATE:(g + 1) * SSD_STATE]
        cb = _dot_nt(c_g, b_g)
        for pair in range(2):
            ms = []
            for hh in range(2):
                lane = doff + 4 * g + 2 * pair + hh
                seg = cum[:, lane:lane + 1] - cum_t[lane:lane + 1, :]
                ms.append((cb * jnp.exp(jnp.where(scanned, seg, NEG_INF))).astype(BF16))
            k = 2 * g + pair
            y_parts.append(_dot(jnp.concatenate(ms, axis=1), _stack_halves(xdt_b[:, k * LANES:(k + 1) * LANES])))
        h_g = h_ref[:, g * gw:(g + 1) * gw]
        y_off = _dot(c_g, h_g.astype(BF16)) * ecum_e[:, g * gw:(g + 1) * gw]
        y_parts[2 * g] = y_parts[2 * g] + y_off[:, :LANES]
        y_parts[2 * g + 1] = y_parts[2 * g + 1] + y_off[:, LANES:]
        new_states.append(h_g * etot_e[:, g * gw:(g + 1) * gw] + _dot_tn(b_g, xw[:, g * gw:(g + 1) * gw]))
    y = jnp.concatenate(y_parts, axis=1)
    h_ref[...] = jnp.concatenate(new_states, axis=1)

    if final:
        y = y + yprev_ref[0].astype(F32) + dskip_ref[...] * xs
        z = z_ref[0].astype(F32)
        y = y * (z * jax.nn.sigmoid(z))
        outs = []
        for g in range(SSD_GROUPS):
            yg = y[:, g * gw:(g + 1) * gw]
            outs.append(yg * lax.rsqrt(jnp.mean(yg * yg, axis=-1, keepdims=True) + NORM_EPS))
        y = jnp.concatenate(outs, axis=1) * nw_ref[...]
    o_ref[0] = y.astype(o_ref.dtype)


def _ssd_pass(zxbc, dt, conv_w, conv_b, dtb_row, alog_row, e_mat, reverse, final_args=None):
    bsz, seq, _ = zxbc.shape
    ln = SSD_CHUNK
    nc = seq // ln
    halo = 16
    hb = ln // halo
    final = final_args is not None

    def cidx(s):
        return (nc - 1 - s) if reverse else s

    def main_spec(col):
        return pl.BlockSpec((1, ln, SSD_INNER), lambda b, s: (b, cidx(s), col))

    def prev_spec(col):
        return pl.BlockSpec((1, halo, SSD_INNER), lambda b, s: (b, jnp.maximum(cidx(s) * hb - 1, 0), col))

    def next_spec(col):
        return pl.BlockSpec((1, halo, SSD_INNER),
                            lambda b, s: (b, jnp.minimum(cidx(s) * hb + hb, seq // halo - 1), col))

    def const_spec(shape, col=0):
        return pl.BlockSpec(shape, lambda b, s: (0, col))

    in_specs = [main_spec(1), prev_spec(1), next_spec(1), main_spec(2), prev_spec(2), next_spec(2),
                pl.BlockSpec((1, ln, LANES), lambda b, s: (b, cidx(s), 0)),
                const_spec((SSD_CONV, SSD_INNER), 0), const_spec((1, SSD_INNER), 0),
                const_spec((SSD_CONV, SSD_INNER), 1), const_spec((1, SSD_INNER), 1),
                const_spec((1, LANES)), const_spec((1, LANES)), const_spec((LANES, SSD_INNER))]
    args = [zxbc, zxbc, zxbc, zxbc, zxbc, zxbc, dt, conv_w, conv_b, conv_w, conv_b, dtb_row, alog_row, e_mat]
    if final:
        yprev, dskip, nw = final_args
        in_specs += [main_spec(0), main_spec(0), const_spec((1, SSD_INNER)), const_spec((1, SSD_INNER))]
        args += [zxbc, yprev, dskip, nw]
    return pl.pallas_call(
        functools.partial(_ssd_kernel, reverse=reverse, final=final),
        grid=(bsz, nc),
        in_specs=in_specs,
        out_specs=main_spec(0),
        out_shape=jax.ShapeDtypeStruct((bsz, seq, SSD_INNER), BF16),
        scratch_shapes=[pltpu.VMEM((SSD_STATE, SSD_INNER), F32)],
        compiler_params=_cparams(("parallel", "arbitrary")),
        name="ssd_bwd_final" if reverse else "ssd_fwd",
    )(*args)


def ssd_expand_matrix(reverse):
    rows = jnp.arange(LANES)[:, None]
    head = jnp.arange(SSD_INNER)[None, :] // HEAD_DIM
    return (rows == head + (SSD_HEADS if reverse else 0)).astype(BF16)


def ssd_mixer(zxbc, dt, conv_w, conv_b, dtb_row, alog_row, dskip, nw):
    y_f = _ssd_pass(zxbc, dt, conv_w, conv_b, dtb_row, alog_row, ssd_expand_matrix(False), reverse=False)
    return _ssd_pass(zxbc, dt, conv_w, conv_b, dtb_row, alog_row, ssd_expand_matrix(True), reverse=True,
                     final_args=(y_f, dskip, nw))


def _merge_kernel(x_ref, nw_ref, wg_ref, ona_ref, oswa_ref, ossd_ref, wna_ref, wswa_ref, wssd_ref, wout_ref,
                  nffn_ref, rhi_ref, rlo_ref, xo_ref, aff_ref):
    x = x_ref[...]
    h = _rmsnorm(x, nw_ref[...]).astype(BF16)
    d = x.shape[1]
    gates = jax.nn.sigmoid(_dot(h, wg_ref[...]))
    merged = (gates[:, :d] * _dot(ona_ref[...], wna_ref[...])
              + gates[:, d:2 * d] * _dot(oswa_ref[...], wswa_ref[...])
              + gates[:, 2 * d:] * _dot(ossd_ref[...], wssd_ref[...]))
    xn = x + _dot(merged.astype(BF16), wout_ref[...])
    xo_ref[...] = xn
    h2 = _rmsnorm(xn, nffn_ref[...])
    h_hi, h_lo = _split2(h2)
    logits = _dot(h_hi, rhi_ref[...]) + _dot(h_hi, rlo_ref[...]) + _dot(h_lo, rhi_ref[...])
    live = lax.broadcasted_iota(I32, logits.shape, 1) < N_EXPERTS
    logits = jnp.where(live, logits, NEG_INF)
    e = jnp.exp(logits - jnp.max(logits, axis=-1, keepdims=True))
    aff_ref[...] = e / jnp.sum(e, axis=-1, keepdims=True)


def merge_and_route(x2d, nw, wg, o_na, o_swa, o_ssd, w_na, w_swa, w_ssd, w_out, nffn, r_hi, r_lo, tm=256):
    t, d = x2d.shape
    tm = min(tm, t)
    row = lambda i: (i, 0)
    fixed = lambda i: (0, 0)
    return pl.pallas_call(
        _merge_kernel,
        grid=(t // tm,),
        in_specs=[pl.BlockSpec((tm, d), row), pl.BlockSpec((1, d), fixed), pl.BlockSpec(wg.shape, fixed),
                  pl.BlockSpec((tm, d), row), pl.BlockSpec((tm, d), row), pl.BlockSpec((tm, d), row),
                  pl.BlockSpec((d, d), fixed), pl.BlockSpec((d, d), fixed), pl.BlockSpec((d, d), fixed),
                  pl.BlockSpec((d, d), fixed), pl.BlockSpec((1, d), fixed),
                  pl.BlockSpec((d, LANES), fixed), pl.BlockSpec((d, LANES), fixed)],
        out_specs=[pl.BlockSpec((tm, d), row), pl.BlockSpec((tm, LANES), row)],
        out_shape=[jax.ShapeDtypeStruct((t, d), F32), jax.ShapeDtypeStruct((t, LANES), F32)],
        compiler_params=_cparams(("parallel",)),
        name="merge_and_route",
    )(x2d, nw, wg, o_na, o_swa, o_ssd, w_na, w_swa, w_ssd, w_out, nffn, r_hi, r_lo)


def _prefix_rows(mask_f, upper, lower_strict):
    lp = _dot(mask_f.astype(BF16), upper)
    rc = jnp.broadcast_to(lp[:, LANES - 1:LANES], lp.shape)
    return lp, rc, _dot(lower_strict, rc.astype(BF16))


def _select_kernel(a_ref, idx_ref, wts_ref, slot_ref, rstart_ref, *, cap, chunk):
    a = a_ref[0]
    nr = a.shape[0]
    bits = pltpu.bitcast(a, I32)

    def search(i, v):
        cand = v | lax.shift_left(jnp.int32(1), 30 - i)
        cnt = jnp.sum((bits >= cand).astype(F32))
        return jnp.where(cnt >= cap, cand, v)

    thr = lax.fori_loop(0, 31, search, jnp.int32(0))
    gt = bits > thr
    eq = bits == thr
    need = cap - jnp.sum(gt.astype(F32))

    li = lax.broadcasted_iota(I32, (LANES, LANES), 0)
    lj = lax.broadcasted_iota(I32, (LANES, LANES), 1)
    upper = (li <= lj).astype(BF16)
    ri = lax.broadcasted_iota(I32, (nr, nr), 0)
    rj = lax.broadcasted_iota(I32, (nr, nr), 1)
    lower_strict = (rj < ri).astype(BF16)

    elp, _, ers = _prefix_rows(eq.astype(F32), upper, lower_strict)
    sel = gt | (eq & (elp + ers <= need))
    lp, rc, rs = _prefix_rows(sel.astype(F32), upper, lower_strict)
    slot_ref[0] = jnp.where(sel, lp + rs - 1.0, -1.0).astype(I32)
    rstart_ref[0] = rs.astype(I32)

    row_end = rs[:, 0:1] + rc[:, 0:1]
    rcount = rc[:, 0:1]
    lp_b = lp.astype(BF16)
    a1, a2, a3 = _split3(a)
    sub_r = lax.broadcasted_iota(I32, (nr, chunk), 0).astype(F32)
    sub_l = lax.broadcasted_iota(I32, (LANES, chunk), 0).astype(F32)
    for ch in range(cap // chunk):
        s = (ch * chunk + lax.broadcasted_iota(I32, (1, chunk), 1)).astype(F32)
        before = (row_end <= s).astype(F32)
        row_of = jnp.sum(before, axis=0, keepdims=True)
        j = s - jnp.sum(before * rcount, axis=0, keepdims=True)
        onehot = (sub_r == row_of).astype(BF16)
        g = _dot_tn(lp_b, onehot)
        lane_of = jnp.sum((g <= j).astype(F32), axis=0, keepdims=True)
        idx_ref[0, :, ch * chunk:(ch + 1) * chunk] = (row_of * LANES + lane_of).astype(I32)
        ga = _dot_tn(a1, onehot) + _dot_tn(a2, onehot) + _dot_tn(a3, onehot)
        wts_ref[0, :, ch * chunk:(ch + 1) * chunk] = jnp.sum(jnp.where(sub_l == lane_of, ga, 0.0), axis=0,
                                                               keepdims=True)


def select_tokens(aff_t, cap):
    ne, nr, _ = aff_t.shape
    chunk = min(1024, cap)
    blk = lambda e: (e, 0, 0)
    return pl.pallas_call(
        functools.partial(_select_kernel, cap=cap, chunk=chunk),
        grid=(ne,),
        in_specs=[pl.BlockSpec((1, nr, LANES), blk)],
        out_specs=[pl.BlockSpec((1, 1, cap), blk), pl.BlockSpec((1, 1, cap), blk),
                   pl.BlockSpec((1, nr, LANES), blk), pl.BlockSpec((1, nr, LANES), blk)],
        out_shape=[jax.ShapeDtypeStruct((ne, 1, cap), I32), jax.ShapeDtypeStruct((ne, 1, cap), F32),
                   jax.ShapeDtypeStruct((ne, nr, LANES), I32), jax.ShapeDtypeStruct((ne, nr, LANES), I32)],
        compiler_params=_cparams(("parallel",)),
        name="select_tokens",
    )(aff_t)


EXPERT_FF_CHUNKS = 4


def _expert_kernel(idx_hbm, x_hbm, w_ref, nffn_ref, wg_ref, wu_ref, wd_ref, o_ref, idx_smem, xbuf, isem, gsem,
                   *, tm):
    nt = pl.num_programs(1)
    n = pl.program_id(0) * nt + pl.program_id(1)
    total = pl.num_programs(0) * nt
    slot = n % 2

    def idx_copy(step, s):
        return pltpu.make_async_copy(idx_hbm.at[step], idx_smem.at[s], isem.at[s])

    def gather_wait(s):
        pltpu.make_async_copy(x_hbm.at[pl.ds(0, tm), :], xbuf.at[s], gsem.at[s]).wait()

    def row_copy(s, r):
        tok = idx_smem[s, r]
        return pltpu.make_async_copy(x_hbm.at[pl.ds(tok, 1), :], xbuf.at[s, pl.ds(r, 1), :], gsem.at[s])

    @pl.when(n == 0)
    def _():
        idx_copy(0, 0).start()
        idx_copy(0, 0).wait()

        def body(r, carry):
            row_copy(0, r).start()
            return carry
        lax.fori_loop(0, tm, body, 0)
        idx_copy(jnp.minimum(1, total - 1), 1).start()

    gather_wait(slot)
    h = _rmsnorm(xbuf[slot], nffn_ref[...]).astype(BF16)
    idx_copy(jnp.minimum(n + 1, total - 1), 1 - slot).wait()
    ff = wg_ref.shape[2]
    nchunk = EXPERT_FF_CHUNKS
    fc, rc = ff // nchunk, tm // nchunk
    y = None
    for c in range(nchunk):
        for r in range(c * rc, (c + 1) * rc):
            row_copy(1 - slot, r).start()
        gate = _dot(h, wg_ref[0, :, c * fc:(c + 1) * fc])
        up = _dot(h, wu_ref[0, :, c * fc:(c + 1) * fc])
        act = (gate * jax.nn.sigmoid(gate) * up).astype(BF16)
        part = _dot(act, wd_ref[0, c * fc:(c + 1) * fc, :])
        y = part if y is None else y + part
    idx_copy(jnp.minimum(n + 2, total - 1), slot).start()
    eye = lax.broadcasted_iota(I32, (LANES, LANES), 0) == lax.broadcasted_iota(I32, (LANES, LANES), 1)
    wcol = jnp.concatenate(
        [jnp.sum(jnp.where(eye, jnp.broadcast_to(w_ref[0, r:r + 1, :], (LANES, LANES)), 0.0), axis=1, keepdims=True)
         for r in range(tm // LANES)], axis=0)
    o_ref[...] = (y * wcol).astype(o_ref.dtype)

    @pl.when(n == total - 1)
    def _():
        gather_wait(1 - slot)
        idx_copy(0, slot).wait()


def run_experts(idx_tiles, x2d, wts_rows, nffn, wg, wu, wd, cap, tm):
    ne = wg.shape[0]
    t, d = x2d.shape
    nt = cap // tm
    ff = wg.shape[2]
    return pl.pallas_call(
        functools.partial(_expert_kernel, tm=tm),
        grid=(ne, nt),
        in_specs=[pl.BlockSpec(memory_space=pl.ANY), pl.BlockSpec(memory_space=pl.ANY),
                  pl.BlockSpec((1, tm // LANES, LANES), lambda e, i: (e * nt + i, 0, 0)),
                  pl.BlockSpec((1, d), lambda e, i: (0, 0)),
                  pl.BlockSpec((1, d, ff), lambda e, i: (e, 0, 0)),
                  pl.BlockSpec((1, d, ff), lambda e, i: (e, 0, 0)),
                  pl.BlockSpec((1, ff, d), lambda e, i: (e, 0, 0))],
        out_specs=pl.BlockSpec((tm, d), lambda e, i: (e * nt + i, 0)),
        out_shape=jax.ShapeDtypeStruct((ne * cap, d), BF16),
        scratch_shapes=[pltpu.SMEM((2, tm), I32), pltpu.VMEM((2, tm, d), F32),
                        pltpu.SemaphoreType.DMA((2,)), pltpu.SemaphoreType.DMA((2,))],
        compiler_params=_cparams(("arbitrary", "arbitrary")),
        name="run_experts",
    )(idx_tiles, x2d, wts_rows, nffn, wg, wu, wd)


COMB_ALIGN = 16
COMB_SMALL = 64
COMB_BIG = LANES + COMB_ALIGN


def _combine_kernel(rstart_ref, x_ref, slot_ref, ye_hbm, o_ref, sbuf, bbuf, ssem, bsem, *, cap):
    r = pl.program_id(0)
    nr = pl.num_programs(0)
    ne = N_EXPERTS
    cur = r % 2

    def start_of(row, e, win):
        p = e * cap + rstart_ref[row * ne + e]
        return jnp.minimum((p // COMB_ALIGN) * COMB_ALIGN, ne * cap - win)

    def fits_small(row):
        ok = None
        for e in range(ne):
            end = e * cap + rstart_ref[(row + 1) * ne + e]
            fit = end - start_of(row, e, COMB_SMALL) <= COMB_SMALL
            ok = fit if ok is None else (ok & fit)
        return ok

    def small_copy(row, e, s):
        st = pl.multiple_of(start_of(row, e, COMB_SMALL), COMB_ALIGN)
        return pltpu.make_async_copy(ye_hbm.at[pl.ds(st, COMB_SMALL), :],
                                     sbuf.at[s, pl.ds(e * COMB_SMALL, COMB_SMALL), :], ssem.at[s, e])

    def big_copy(row, e, s):
        st = pl.multiple_of(start_of(row, e, COMB_BIG), COMB_ALIGN)
        return pltpu.make_async_copy(ye_hbm.at[pl.ds(st, COMB_BIG), :], bbuf.at[s, e], bsem.at[s, e])

    def issue(row, s):
        small = fits_small(row)

        @pl.when(small)
        def _():
            for e in range(ne):
                small_copy(row, e, s).start()

        @pl.when(jnp.logical_not(small))
        def _():
            for e in range(ne):
                big_copy(row, e, s).start()

    @pl.when(r == 0)
    def _():
        issue(0, 0)

    @pl.when(r + 1 < nr)
    def _():
        issue(r + 1, 1 - cur)

    small = fits_small(r)
    slots = slot_ref[...]

    def rel(e, win, shift):
        sl = slots[:, e:e + 1]
        return jnp.where(sl >= 0, sl + (e * cap + shift - start_of(r, e, win)), -1)

    @pl.when(small)
    def _():
        for e in range(ne):
            small_copy(r, e, cur).wait()
        lane = lax.broadcasted_iota(I32, (LANES, LANES), 1)
        pieces = [(lane == jnp.where(lane < COMB_SMALL, rel(2 * k, COMB_SMALL, 0),
                                     rel(2 * k + 1, COMB_SMALL, COMB_SMALL))).astype(BF16)
                  for k in range(ne // 2)]
        o_ref[...] = x_ref[...] + _dot(jnp.concatenate(pieces, axis=1), sbuf[cur])

    @pl.when(jnp.logical_not(small))
    def _():
        acc = x_ref[...]
        lane = lax.broadcasted_iota(I32, (LANES, COMB_BIG), 1)
        for e in range(ne):
            big_copy(r, e, cur).wait()
            acc = acc + _dot((lane == rel(e, COMB_BIG, 0)).astype(BF16), bbuf[cur, e])
        o_ref[...] = acc


def combine(x2d, slot_t, rstart_flat, ye, cap):
    t, d = x2d.shape
    nr = t // LANES
    ne = N_EXPERTS
    grid_spec = pltpu.PrefetchScalarGridSpec(
        num_scalar_prefetch=1,
        grid=(nr,),
        in_specs=[pl.BlockSpec((LANES, d), lambda r, rs: (r, 0)),
                  pl.BlockSpec((LANES, ne), lambda r, rs: (r, 0)),
                  pl.BlockSpec(memory_space=pl.ANY)],
        out_specs=pl.BlockSpec((LANES, d), lambda r, rs: (r, 0)),
        scratch_shapes=[pltpu.VMEM((2, ne * COMB_SMALL, d), BF16), pltpu.VMEM((2, ne, COMB_BIG, d), BF16),
                        pltpu.SemaphoreType.DMA((2, ne)), pltpu.SemaphoreType.DMA((2, ne))],
    )
    return pl.pallas_call(
        functools.partial(_combine_kernel, cap=cap),
        grid_spec=grid_spec,
        out_shape=jax.ShapeDtypeStruct((t, d), F32),
        compiler_params=_cparams(("arbitrary",)),
        name="combine",
    )(rstart_flat, x2d, slot_t, ye)


def moe_layer(x2d, aff, nffn, wg, wu, wd):
    t, d = x2d.shape
    ne = N_EXPERTS
    cap = CAPACITY_FACTOR * t // ne
    nr = t // LANES
    tm = min(512, cap)
    aff_t = aff[:, :ne].T.reshape(ne, nr, LANES)
    idx, wts, slot, rstart = select_tokens(aff_t, cap)
    ye = run_experts(idx.reshape(ne * cap // tm, tm), x2d, wts.reshape(ne * cap // tm, tm // LANES, LANES), nffn,
                     wg, wu, wd, cap, tm)
    slot_t = slot.reshape(ne, t).T
    rstart_flat = jnp.concatenate([rstart[:, :, 0].T, jnp.full((1, ne), cap, I32)], axis=0).reshape((nr + 1) * ne)
    return combine(x2d, slot_t, rstart_flat, ye, cap)


def _prep_layer(p, l):
    w_in = p["w_in"][l]
    d = D_MODEL
    c_na, c_swa = 3 * d, d + 2 * SWA_KV_HEADS * HEAD_DIM
    o_ssd = c_na + c_swa
    c_ssd = d + (d + 2 * SSD_GROUPS * SSD_STATE)
    o_dt = o_ssd + c_ssd
    o_gate = o_dt + 2 * SSD_HEADS
    w_dt = jnp.pad(w_in[:, o_dt:o_gate], ((0, 0), (0, LANES - 2 * SSD_HEADS)))
    pad_row = lambda v: jnp.pad(v.astype(F32).reshape(-1), (0, LANES - 2 * SSD_HEADS))[None, :]
    r_hi, r_lo = _split2(jnp.pad(p["router_w"][l], ((0, 0), (0, LANES - N_EXPERTS))))
    scale = HEAD_DIM ** -0.5
    head_row = lambda qn, kn, nq, nk: jnp.concatenate(
        [jnp.tile(qn.astype(F32) * scale, nq), jnp.tile(kn.astype(F32), nk)])[None, :]
    w_swa = w_in[:, c_na:o_ssd]
    dup = lambda w: jnp.repeat(w.reshape(d, SWA_KV_HEADS, 1, HEAD_DIM), 2, axis=2).reshape(d, 2 * w.shape[1])
    kv_w = SWA_KV_HEADS * HEAD_DIM
    w_swa = jnp.concatenate([w_swa[:, :d], dup(w_swa[:, d:d + kv_w]), dup(w_swa[:, d + kv_w:])], axis=1)
    return dict(
        norm_mix=p["norm_mix"][l][None, :],
        w_na_in=w_in[:, :c_na].astype(BF16),
        w_swa_in=w_swa.astype(BF16),
        w_ssd_in=w_in[:, o_ssd:o_dt].astype(BF16),
        w_dt=w_dt.astype(BF16),
        w_gate=w_in[:, o_gate:].astype(BF16),
        na_heads=head_row(p["na_q_norm"][l], p["na_k_norm"][l], NA_HEADS, NA_HEADS),
        na_tab=na_bias_table(p["na_rel_bias"][l]),
        swa_heads=head_row(p["swa_q_norm"][l], p["swa_k_norm"][l], SWA_Q_HEADS, 2 * SWA_KV_HEADS),
        swa_sink=p["swa_sink"][l].astype(F32),
        conv_w=p["ssd_conv_w"][l].astype(F32), conv_b=p["ssd_conv_b"][l].astype(F32)[None, :],
        dtb_row=pad_row(p["ssd_dt_bias"][l]), alog_row=pad_row(p["ssd_a_log"][l]),
        dskip=jnp.repeat(p["ssd_d"][l].astype(F32), HEAD_DIM)[None, :],
        ssd_norm=p["ssd_norm"][l].astype(F32)[None, :],
        w_na=p["w_branch_na"][l].astype(BF16), w_swa=p["w_branch_swa"][l].astype(BF16),
        w_ssd=p["w_branch_ssd"][l].astype(BF16), w_out=p["w_out"][l].astype(BF16),
        norm_ffn=p["norm_ffn"][l][None, :], r_hi=r_hi, r_lo=r_lo,
        wg=p["expert_w_gate"][l].astype(BF16), wu=p["expert_w_up"][l].astype(BF16),
        wd=p["expert_w_down"][l].astype(BF16),
    )


def _layer(x, lp, cos, sin):
    bsz, seq, d = x.shape
    t = bsz * seq
    x2d = x.reshape(t, d)
    qkv_na = norm_proj_heads(x2d, lp["norm_mix"], lp["w_na_in"], lp["na_heads"], seq)
    qkv_swa = norm_proj_heads(x2d, lp["norm_mix"], lp["w_swa_in"], lp["swa_heads"], seq, cos, sin)
    zxbc, dt = norm_proj(x2d, lp["norm_mix"], [lp["w_ssd_in"], lp["w_dt"]], [BF16, F32])
    o_na = neighborhood_attention(qkv_na.reshape(bsz, seq, -1), lp["na_tab"])
    o_swa = sliding_window_attention(qkv_swa.reshape(bsz, seq, -1), lp["swa_sink"])
    o_ssd = ssd_mixer(zxbc.reshape(bsz, seq, -1), dt.reshape(bsz, seq, -1), lp["conv_w"], lp["conv_b"],
                      lp["dtb_row"], lp["alog_row"], lp["dskip"], lp["ssd_norm"])
    x_mid, aff = merge_and_route(x2d, lp["norm_mix"], lp["w_gate"], o_na.reshape(t, d), o_swa.reshape(t, d),
                                 o_ssd.reshape(t, d), lp["w_na"], lp["w_swa"], lp["w_ssd"], lp["w_out"],
                                 lp["norm_ffn"], lp["r_hi"], lp["r_lo"])
    return moe_layer(x_mid, aff, lp["norm_ffn"], lp["wg"], lp["wu"], lp["wd"]).reshape(bsz, seq, d)


def kernel(x_prompt, x_sample, norm_mix, w_in, na_q_norm, na_k_norm, na_rel_bias, swa_q_norm, swa_k_norm, swa_sink, ssd_conv_w, ssd_conv_b, ssd_dt_bias, ssd_a_log, ssd_d, ssd_norm, w_branch_na, w_branch_swa, w_branch_ssd, w_out, norm_ffn, router_w, expert_w_gate, expert_w_up, expert_w_down):
    p = dict(norm_mix=norm_mix, w_in=w_in, na_q_norm=na_q_norm, na_k_norm=na_k_norm, na_rel_bias=na_rel_bias,
             swa_q_norm=swa_q_norm, swa_k_norm=swa_k_norm, swa_sink=swa_sink, ssd_conv_w=ssd_conv_w,
             ssd_conv_b=ssd_conv_b, ssd_dt_bias=ssd_dt_bias, ssd_a_log=ssd_a_log, ssd_d=ssd_d, ssd_norm=ssd_norm,
             w_branch_na=w_branch_na, w_branch_swa=w_branch_swa, w_branch_ssd=w_branch_ssd, w_out=w_out,
             norm_ffn=norm_ffn, router_w=router_w, expert_w_gate=expert_w_gate, expert_w_up=expert_w_up,
             expert_w_down=expert_w_down)
    depth = w_in.shape[0]
    layers = [_prep_layer(p, l) for l in range(depth)]
    outs = []
    for x in (x_prompt, x_sample):
        cos, sin = rope_tables(x.shape[1])
        for lp in layers:
            x = _layer(x, lp, cos, sin)
        outs.append(x)
    return tuple(outs)
```

```python
import functools
import math

import jax
import jax.numpy as jnp
from jax import lax
from jax.experimental import pallas as pl
from jax.experimental.pallas import tpu as pltpu

F32 = jnp.float32
BF16 = jnp.bfloat16
I32 = jnp.int32

D_MODEL = 1024
HEAD_DIM = 64
GRID_W = 64
NA_HEADS = 16
NA_WIN_ROWS = 8
NA_WIN_COLS = 16
NA_QROWS = 4
NA_KROWS = 12
SWA_Q_HEADS = 16
SWA_KV_HEADS = 4
SWA_WINDOW = 128
SWA_BLOCK = 128
ROPE_THETA = 10000.0
SSD_HEADS = 16
SSD_GROUPS = 4
SSD_STATE = 128
SSD_CONV = 4
SSD_CHUNK = 128
SSD_INNER = 1024
N_EXPERTS = 16
EXPERT_FF = 2048
CAPACITY_FACTOR = 2
NORM_EPS = 1e-6
NEG_INF = -1e30
LANES = 128
VMEM_LIMIT = 56 * 1024 * 1024


def _cparams(sem):
    return pltpu.CompilerParams(dimension_semantics=sem, vmem_limit_bytes=VMEM_LIMIT)


def _rmsnorm(x, w):
    return x * lax.rsqrt(jnp.mean(x * x, axis=-1, keepdims=True) + NORM_EPS) * w


def _lo_mask(shape):
    return lax.broadcasted_iota(I32, shape, len(shape) - 1) < HEAD_DIM


def _headnorm2(x, w):
    lo = _lo_mask(x.shape)
    x2 = x * x
    sa = jnp.sum(jnp.where(lo, x2, 0.0), axis=-1, keepdims=True)
    sb = jnp.sum(jnp.where(lo, 0.0, x2), axis=-1, keepdims=True)
    ms = jnp.where(lo, sa, sb) * (1.0 / HEAD_DIM)
    return x * lax.rsqrt(ms + NORM_EPS) * w


def _stack_halves(x):
    lo = _lo_mask(x.shape)
    zero = jnp.zeros_like(x)
    return jnp.concatenate([jnp.where(lo, x, zero), jnp.where(lo, zero, x)], axis=0)


def _dot(a, b):
    return jnp.dot(a, b, preferred_element_type=F32)


def _dot_nt(a, b):
    return lax.dot_general(a, b, (((1,), (1,)), ((), ())), preferred_element_type=F32)


def _dot_tn(a, b):
    return lax.dot_general(a, b, (((0,), (0,)), ((), ())), preferred_element_type=F32)


def _split2(x):
    hi = x.astype(BF16)
    lo = (x - hi.astype(F32)).astype(BF16)
    return hi, lo


def _split3(x):
    a = x.astype(BF16)
    r = x - a.astype(F32)
    b = r.astype(BF16)
    c = (r - b.astype(F32)).astype(BF16)
    return a, b, c


def _norm_proj_kernel(*refs, n_out):
    x_ref, nw_ref = refs[0], refs[1]
    w_refs = refs[2:2 + n_out]
    o_refs = refs[2 + n_out:2 + 2 * n_out]
    h = _rmsnorm(x_ref[...], nw_ref[...]).astype(BF16)
    for w_ref, o_ref in zip(w_refs, o_refs):
        o_ref[...] = _dot(h, w_ref[...]).astype(o_ref.dtype)


def norm_proj(x2d, nw, weights, out_dtypes, tm=512):
    t, d = x2d.shape
    tm = min(tm, t)
    n_out = len(weights)
    return pl.pallas_call(
        functools.partial(_norm_proj_kernel, n_out=n_out),
        grid=(t // tm,),
        in_specs=[pl.BlockSpec((tm, d), lambda i: (i, 0)), pl.BlockSpec((1, d), lambda i: (0, 0))]
        + [pl.BlockSpec(w.shape, lambda i: (0, 0)) for w in weights],
        out_specs=[pl.BlockSpec((tm, w.shape[1]), lambda i: (i, 0)) for w in weights],
        out_shape=[jax.ShapeDtypeStruct((t, w.shape[1]), dt) for w, dt in zip(weights, out_dtypes)],
        compiler_params=_cparams(("parallel",)),
        name="norm_proj",
    )(x2d, nw, *weights)


def _rope(x, cos, sin_signed):
    first = (lax.broadcasted_iota(I32, x.shape, 1) % HEAD_DIM) < HEAD_DIM // 2
    partner = jnp.where(first, pltpu.roll(x, LANES - HEAD_DIM // 2, axis=1), pltpu.roll(x, HEAD_DIM // 2, axis=1))
    return x * cos + partner * sin_signed


PROJ_CHUNK = 512


def _norm_proj_heads_kernel(*refs, n_norm, rope):
    if rope:
        x_ref, nw_ref, w_ref, hw_ref, cos_ref, sin_ref, o_ref = refs
    else:
        x_ref, nw_ref, w_ref, hw_ref, o_ref = refs
    h = _rmsnorm(x_ref[...], nw_ref[...]).astype(BF16)
    ncols = w_ref.shape[1]
    for c0 in range(0, ncols, PROJ_CHUNK):
        acc = _dot(h, w_ref[:, c0:c0 + PROJ_CHUNK])
        parts = []
        for j in range(PROJ_CHUNK // LANES):
            col = c0 // LANES + j
            blk = acc[:, j * LANES:(j + 1) * LANES]
            if col < n_norm:
                blk = _headnorm2(blk, hw_ref[:, col * LANES:(col + 1) * LANES])
                if rope:
                    blk = _rope(blk, cos_ref[...], sin_ref[...])
            parts.append(blk)
        o_ref[:, c0:c0 + PROJ_CHUNK] = jnp.concatenate(parts, axis=1).astype(o_ref.dtype)


def norm_proj_heads(x2d, nw, w, head_w, seq, cos=None, sin=None, tm=512):
    t, d = x2d.shape
    tm = min(tm, t, seq)
    n = w.shape[1]
    n_norm = head_w.shape[1] // LANES
    rope = cos is not None
    in_specs = [pl.BlockSpec((tm, d), lambda i: (i, 0)), pl.BlockSpec((1, d), lambda i: (0, 0)),
                pl.BlockSpec(w.shape, lambda i: (0, 0)), pl.BlockSpec(head_w.shape, lambda i: (0, 0))]
    args = [x2d, nw, w, head_w]
    if rope:
        per_seq = seq // tm
        in_specs += [pl.BlockSpec((tm, LANES), lambda i: (i % per_seq, 0))] * 2
        args += [cos, sin]
    return pl.pallas_call(
        functools.partial(_norm_proj_heads_kernel, n_norm=n_norm, rope=rope),
        grid=(t // tm,),
        in_specs=in_specs,
        out_specs=pl.BlockSpec((tm, n), lambda i: (i, 0)),
        out_shape=jax.ShapeDtypeStruct((t, n), BF16),
        compiler_params=_cparams(("parallel",)),
        name="norm_proj_heads",
    )(*args)


NA_PAIRS = 4


def _na_kernel(q_ref, k0_ref, k1_ref, k2_ref, v0_ref, v1_ref, v2_ref, tab_ref, o_ref):
    g = pl.program_id(2)
    ng = pl.num_programs(2)
    variant = jnp.where(g == 0, 0, jnp.where(g == ng - 1, 2, 1))
    nk = NA_KROWS * GRID_W
    outs = []
    for hp in range(NA_PAIRS):
        cols = slice(hp * LANES, (hp + 1) * LANES)
        kw = jnp.concatenate([k0_ref[0, :, cols], k1_ref[0, :, cols], k2_ref[0, :, cols]], axis=0)
        s = _dot_nt(q_ref[0, :, cols], _stack_halves(kw)) + tab_ref[hp, variant]
        sa, sb = s[:, :nk], s[:, nk:]
        pa = jnp.exp(sa - jnp.max(sa, axis=-1, keepdims=True))
        pb = jnp.exp(sb - jnp.max(sb, axis=-1, keepdims=True))
        la = jnp.sum(pa, axis=-1, keepdims=True)
        lb = jnp.sum(pb, axis=-1, keepdims=True)
        p = jnp.concatenate([pa, pb], axis=1).astype(BF16)
        vw = jnp.concatenate([v0_ref[0, :, cols], v1_ref[0, :, cols], v2_ref[0, :, cols]], axis=0)
        o = _dot(p, _stack_halves(vw))
        outs.append(o * jnp.where(_lo_mask(o.shape), 1.0 / la, 1.0 / lb))
    o_ref[0] = jnp.concatenate(outs, axis=1).astype(o_ref.dtype)


def na_bias_table(rel_bias):
    nh = rel_bias.shape[0]
    offs = jnp.array([0, NA_QROWS, 2 * NA_QROWS], I32)
    r0w = jnp.array([[0, 0, 0, 0], [0, 1, 2, 3], [4, 4, 4, 4]], I32)
    j = jnp.arange(NA_QROWS, dtype=I32)
    i = jnp.arange(NA_KROWS, dtype=I32)
    c = jnp.arange(GRID_W, dtype=I32)
    qrow = offs[:, None] + j[None, :]
    row_rel = i[None, None, :] - qrow[:, :, None] + NA_WIN_ROWS - 1
    row_ok = (i[None, None, :] >= r0w[:, :, None]) & (i[None, None, :] < r0w[:, :, None] + NA_WIN_ROWS)
    c0 = jnp.clip(c - NA_WIN_COLS // 2, 0, GRID_W - NA_WIN_COLS)
    col_rel = c[None, :] - c[:, None] + NA_WIN_COLS - 1
    col_ok = (c[None, :] >= c0[:, None]) & (c[None, :] < c0[:, None] + NA_WIN_COLS)
    row_sel = (row_rel[..., None] == jnp.arange(2 * NA_WIN_ROWS - 1)).astype(F32)
    col_sel = (col_rel[..., None] == jnp.arange(2 * NA_WIN_COLS - 1)).astype(F32)
    by_col = jnp.einsum('hab,ckb->hack', rel_bias.astype(F32), col_sel, precision=lax.Precision.HIGHEST)
    b = jnp.einsum('vjia,hack->hvjcik', row_sel, by_col, precision=lax.Precision.HIGHEST)
    ok = row_ok[:, :, None, :, None] & col_ok[None, None, :, None, :]
    b = jnp.where(ok[None], b, NEG_INF)
    b = b.reshape(nh // 2, 2, 3, NA_QROWS * GRID_W, NA_KROWS * GRID_W)
    return jnp.concatenate([b[:, 0], b[:, 1]], axis=-1)


def neighborhood_attention(qkv, tab):
    bsz, seq, _ = qkv.shape
    rows = seq // GRID_W
    assert rows % NA_QROWS == 0 and rows >= NA_KROWS + NA_QROWS
    ng = rows // NA_QROWS
    tq = NA_QROWS * GRID_W
    wblk = NA_PAIRS * LANES
    nblk = NA_HEADS * HEAD_DIM // wblk

    def kv_spec(col0, j):
        return pl.BlockSpec((1, tq, wblk), lambda hp, b, g: (b, jnp.clip(g - 1, 0, ng - 3) + j, col0 + hp))

    return pl.pallas_call(
        _na_kernel,
        grid=(nblk, bsz, ng),
        in_specs=[pl.BlockSpec((1, tq, wblk), lambda hp, b, g: (b, g, hp))]
        + [kv_spec(nblk, j) for j in range(3)] + [kv_spec(2 * nblk, j) for j in range(3)]
        + [pl.BlockSpec((NA_PAIRS, 3, tq, 2 * NA_KROWS * GRID_W), lambda hp, b, g: (hp, 0, 0, 0),
                        pipeline_mode=pl.Buffered(1))],
        out_specs=pl.BlockSpec((1, tq, wblk), lambda hp, b, g: (b, g, hp)),
        out_shape=jax.ShapeDtypeStruct((bsz, seq, NA_HEADS * HEAD_DIM), BF16),
        compiler_params=_cparams(("parallel", "parallel", "parallel")),
        name="neighborhood_attention",
    )(qkv, qkv, qkv, qkv, qkv, qkv, qkv, tab)


def _swa_kernel(sink_ref, q_ref, kp_ref, kc_ref, kx_ref, vp_ref, vc_ref, vx_ref, o_ref, *, seq):
    i = pl.program_id(1)
    blk = SWA_BLOCK
    span = 3 * blk
    qpos = i * blk + lax.broadcasted_iota(I32, (2 * blk, span), 0) % blk
    kpos = (i - 1) * blk + lax.broadcasted_iota(I32, (2 * blk, span), 1)
    valid = (jnp.abs(kpos - qpos) <= SWA_WINDOW) & (kpos >= 0) & (kpos < seq)
    upper = lax.broadcasted_iota(I32, (2 * blk, 1), 0) < blk
    group = SWA_Q_HEADS // SWA_KV_HEADS
    qw = group * HEAD_DIM

    def softmax_part(sc, sink):
        sc = jnp.where(valid, sc, NEG_INF)
        m = jnp.maximum(jnp.max(sc, axis=-1, keepdims=True), sink)
        e = jnp.exp(sc - m)
        return e, jnp.sum(e, axis=-1, keepdims=True) + jnp.exp(sink - m)

    outs = []
    for kvh in range(SWA_KV_HEADS):
        q = q_ref[0, :, kvh * qw:(kvh + 1) * qw]
        q2 = jnp.concatenate([q[:, :LANES], q[:, LANES:]], axis=0)
        cols = slice(kvh * LANES, (kvh + 1) * LANES)
        kk = jnp.concatenate([kp_ref[0, :, cols], kc_ref[0, :, cols], kx_ref[0, :, cols]], axis=0)
        s = _dot_nt(q2, _stack_halves(kk))
        h0 = group * kvh
        ea, la = softmax_part(s[:, :span], jnp.where(upper, sink_ref[h0], sink_ref[h0 + 2]))
        eb, lb = softmax_part(s[:, span:], jnp.where(upper, sink_ref[h0 + 1], sink_ref[h0 + 3]))
        p = jnp.concatenate([ea, eb], axis=1).astype(BF16)
        vv = jnp.concatenate([vp_ref[0, :, cols], vc_ref[0, :, cols], vx_ref[0, :, cols]], axis=0)
        o = _dot(p, _stack_halves(vv))
        o = o * jnp.where(_lo_mask(o.shape), 1.0 / la, 1.0 / lb)
        outs.append(jnp.concatenate([o[:blk], o[blk:]], axis=1))
    o_ref[0] = jnp.concatenate(outs, axis=1).astype(o_ref.dtype)


def rope_tables(seq):
    half = HEAD_DIM // 2
    inv = ROPE_THETA ** (-jnp.arange(half, dtype=F32) / half)
    ang = jnp.arange(seq, dtype=F32)[:, None] * inv[None, :]
    cos, sin = jnp.cos(ang), jnp.sin(ang)
    return jnp.tile(cos, (1, 4)), jnp.tile(jnp.concatenate([-sin, sin], axis=1), (1, 2))


def sliding_window_attention(qkv, sink):
    bsz, seq, _ = qkv.shape
    blk = SWA_BLOCK
    nblk = seq // blk
    qcols = SWA_Q_HEADS * HEAD_DIM
    kvw = SWA_KV_HEADS * LANES

    def kv_spec(col, j):
        return pl.BlockSpec((1, blk, kvw), lambda b, i: (b, jnp.clip(i + j - 1, 0, nblk - 1), col))

    return pl.pallas_call(
        functools.partial(_swa_kernel, seq=seq),
        grid=(bsz, nblk),
        in_specs=[pl.BlockSpec(memory_space=pltpu.SMEM), pl.BlockSpec((1, blk, qcols), lambda b, i: (b, i, 0))]
        + [kv_spec(qcols // kvw, j) for j in range(3)] + [kv_spec(qcols // kvw + 1, j) for j in range(3)],
        out_specs=pl.BlockSpec((1, blk, qcols), lambda b, i: (b, i, 0)),
        out_shape=jax.ShapeDtypeStruct((bsz, seq, qcols), BF16),
        compiler_params=_cparams(("parallel", "parallel")),
        name="sliding_window_attention",
    )(sink, qkv, qkv, qkv, qkv, qkv, qkv, qkv)


SSD_STEP_CHUNKS = 2
SSD_HALO = 16


def _conv_silu(main_ref, prev_ref, next_ref, w_ref, b_ref, has_prev, has_next):
    xm = main_ref[0].astype(F32)
    n = xm.shape[0]
    pv = prev_ref[0].astype(F32) * has_prev
    nx = next_ref[0].astype(F32) * has_next
    hp = pv.shape[0]
    r8 = lax.broadcasted_iota(I32, (8, 1), 0)

    def shifted(k, edge):
        rolled = pltpu.roll(xm, k % n, axis=0)
        if k > 0:
            return jnp.concatenate([edge(rolled[:8]), rolled[8:]], axis=0)
        return jnp.concatenate([rolled[:n - 8], edge(rolled[n - 8:])], axis=0)

    xm1 = shifted(1, lambda t: jnp.where(r8 == 0, pv[hp - 1:hp], t))
    xm2 = shifted(2, lambda t: jnp.where(r8 == 0, pv[hp - 2:hp - 1], jnp.where(r8 == 1, pv[hp - 1:hp], t)))
    xp1 = shifted(-1, lambda t: jnp.where(r8 == 7, nx[0:1], t))
    w = w_ref[...]
    y = w[0:1] * xm2 + w[1:2] * xm1 + w[2:3] * xm + w[3:4] * xp1 + b_ref[...]
    return y * jax.nn.sigmoid(y)


def _ssd_chunk(xs, bc, dt_raw, h, dtb, alog, e_mat, reverse):
    ln = SSD_CHUNK
    doff = SSD_HEADS if reverse else 0
    gw = SSD_INNER // SSD_GROUPS
    dt = jax.nn.softplus(dt_raw + dtb)
    a_dt = dt * (-jnp.exp(alog))
    tr = lax.broadcasted_iota(I32, (ln, ln), 0)
    tc = lax.broadcasted_iota(I32, (ln, ln), 1)
    scanned = (tc >= tr) if reverse else (tc <= tr)
    tri = scanned.astype(BF16)
    a1, a2, a3 = _split3(a_dt)
    cum = _dot(tri, a1) + _dot(tri, a2) + _dot(tri, a3)
    tot = jnp.sum(a_dt, axis=0, keepdims=True)
    cum_t = cum.T

    stacked = jnp.concatenate([dt, jnp.exp(tot - cum), jnp.exp(cum), jnp.broadcast_to(jnp.exp(tot), (8, LANES))],
                              axis=0)
    s_hi, s_lo = _split2(stacked)
    expanded = _dot(s_hi, e_mat) + _dot(s_lo, e_mat)
    dt_e, wst_e, ecum_e, etot_e = (expanded[:ln], expanded[ln:2 * ln], expanded[2 * ln:3 * ln],
                                   expanded[3 * ln:3 * ln + 1])
    xdt = xs * dt_e
    xw = (xdt * wst_e).astype(BF16)
    xdt_b = xdt.astype(BF16)

    bm = bc[:, :SSD_GROUPS * SSD_STATE].astype(BF16)
    cm = bc[:, SSD_GROUPS * SSD_STATE:].astype(BF16)
    y_parts = []
    new_states = []
    for g in range(SSD_GROUPS):
        b_g = bm[:, g * SSD_STATE:(g + 1) * SSD_STATE]
        c_g = cm[:, g * SSD_STATE:(g + 1) * SSD_STATE]
        cb = _dot_nt(c_g, b_g)
        h_g = h[:, g * gw:(g + 1) * gw]
        y_off = _dot(c_g, h_g.astype(BF16)) * ecum_e[:, g * gw:(g + 1) * gw]
        for pair in range(2):
            ms = []
            for hh in range(2):
                lane = doff + 4 * g + 2 * pair + hh
                seg = cum[:, lane:lane + 1] - cum_t[lane:lane + 1, :]
                ms.append((cb * jnp.exp(jnp.where(scanned, seg, NEG_INF))).astype(BF16))
            k = 2 * g + pair
            y_parts.append(_dot(jnp.concatenate(ms, axis=1), _stack_halves(xdt_b[:, k * LANES:(k + 1) * LANES]))
                           + y_off[:, pair * LANES:(pair + 1) * LANES])
        new_states.append(h_g * etot_e[:, g * gw:(g + 1) * gw] + _dot_tn(b_g, xw[:, g * gw:(g + 1) * gw]))
    return jnp.concatenate(y_parts, axis=1), jnp.concatenate(new_states, axis=1)


def _ssd_kernel(*refs, reverse):
    if reverse:
        (xs_ref, bc_ref, dt_ref, dtb_ref, alog_ref, e_ref, z_ref, yprev_ref, dskip_ref, nw_ref,
         o_ref, h_ref) = refs
    else:
        (x_ref, xp_ref, xn_ref, bc_ref, bcp_ref, bcn_ref, dt_ref, cwx_ref, cbx_ref, cwb_ref, cbb_ref,
         dtb_ref, alog_ref, e_ref, o_ref, xs_out_ref, bc_out_ref, h_ref) = refs
    step = pl.program_id(1)
    nb = pl.num_programs(1)
    ln = SSD_CHUNK
    gw = SSD_INNER // SSD_GROUPS

    @pl.when(step == 0)
    def _():
        h_ref[...] = jnp.zeros_like(h_ref)

    if reverse:
        xs_all = xs_ref[0].astype(F32)
        bc_all = bc_ref[0].astype(F32)
    else:
        has_prev = (step > 0).astype(F32)
        has_next = (step < nb - 1).astype(F32)
        xs_all = _conv_silu(x_ref, xp_ref, xn_ref, cwx_ref, cbx_ref, has_prev, has_next)
        bc_all = _conv_silu(bc_ref, bcp_ref, bcn_ref, cwb_ref, cbb_ref, has_prev, has_next)
        xs_out_ref[0] = xs_all.astype(xs_out_ref.dtype)
        bc_out_ref[0] = bc_all.astype(bc_out_ref.dtype)

    h = h_ref[...]
    ys = [None] * SSD_STEP_CHUNKS
    order = range(SSD_STEP_CHUNKS - 1, -1, -1) if reverse else range(SSD_STEP_CHUNKS)
    for ci in order:
        rows = slice(ci * ln, (ci + 1) * ln)
        ys[ci], h = _ssd_chunk(xs_all[rows], bc_all[rows], dt_ref[0, rows, :], h, dtb_ref[...], alog_ref[...],
                               e_ref[...], reverse)
    h_ref[...] = h
    y = jnp.concatenate(ys, axis=0)

    if reverse:
        y = y + yprev_ref[0].astype(F32) + dskip_ref[...] * xs_all
        z = z_ref[0].astype(F32)
        y = y * (z * jax.nn.sigmoid(z))
        outs = []
        for g in range(SSD_GROUPS):
            yg = y[:, g * gw:(g + 1) * gw]
            outs.append(yg * lax.rsqrt(jnp.mean(yg * yg, axis=-1, keepdims=True) + NORM_EPS))
        y = jnp.concatenate(outs, axis=1) * nw_ref[...]
    o_ref[0] = y.astype(o_ref.dtype)


def _ssd_forward(zxbc, dt, conv_w, conv_b, dtb_row, alog_row):
    bsz, seq, _ = zxbc.shape
    rows = SSD_STEP_CHUNKS * SSD_CHUNK
    nb = seq // rows
    hb = rows // SSD_HALO

    def main_spec(col):
        return pl.BlockSpec((1, rows, SSD_INNER), lambda b, s: (b, s, col))

    def prev_spec(col):
        return pl.BlockSpec((1, SSD_HALO, SSD_INNER), lambda b, s: (b, jnp.maximum(s * hb - 1, 0), col))

    def next_spec(col):
        return pl.BlockSpec((1, SSD_HALO, SSD_INNER),
                            lambda b, s: (b, jnp.minimum(s * hb + hb, seq // SSD_HALO - 1), col))

    def const_spec(shape, col=0):
        return pl.BlockSpec(shape, lambda b, s: (0, col))

    act = jax.ShapeDtypeStruct((bsz, seq, SSD_INNER), BF16)
    return pl.pallas_call(
        functools.partial(_ssd_kernel, reverse=False),
        grid=(bsz, nb),
        in_specs=[main_spec(1), prev_spec(1), next_spec(1), main_spec(2), prev_spec(2), next_spec(2),
                  pl.BlockSpec((1, rows, LANES), lambda b, s: (b, s, 0)),
                  const_spec((SSD_CONV, SSD_INNER), 0), const_spec((1, SSD_INNER), 0),
                  const_spec((SSD_CONV, SSD_INNER), 1), const_spec((1, SSD_INNER), 1),
                  const_spec((1, LANES)), const_spec((1, LANES)), const_spec((LANES, SSD_INNER))],
        out_specs=[main_spec(0)] * 3,
        out_shape=[act, act, act],
        scratch_shapes=[pltpu.VMEM((SSD_STATE, SSD_INNER), F32)],
        compiler_params=_cparams(("parallel", "arbitrary")),
        name="ssd_fwd",
    )(zxbc, zxbc, zxbc, zxbc, zxbc, zxbc, dt, conv_w, conv_b, conv_w, conv_b, dtb_row, alog_row,
      ssd_expand_matrix(False))


def _ssd_backward(zxbc, xs, bc, dt, dtb_row, alog_row, y_fwd, dskip, nw):
    bsz, seq, _ = zxbc.shape
    rows = SSD_STEP_CHUNKS * SSD_CHUNK
    nb = seq // rows
    main = pl.BlockSpec((1, rows, SSD_INNER), lambda b, s: (b, nb - 1 - s, 0))

    def const_spec(shape):
        return pl.BlockSpec(shape, lambda b, s: (0, 0))

    return pl.pallas_call(
        functools.partial(_ssd_kernel, reverse=True),
        grid=(bsz, nb),
        in_specs=[main, main, pl.BlockSpec((1, rows, LANES), lambda b, s: (b, nb - 1 - s, 0)),
                  const_spec((1, LANES)), const_spec((1, LANES)), const_spec((LANES, SSD_INNER)),
                  main, main, const_spec((1, SSD_INNER)), const_spec((1, SSD_INNER))],
        out_specs=main,
        out_shape=jax.ShapeDtypeStruct((bsz, seq, SSD_INNER), BF16),
        scratch_shapes=[pltpu.VMEM((SSD_STATE, SSD_INNER), F32)],
        compiler_params=_cparams(("parallel", "arbitrary")),
        name="ssd_bwd_final",
    )(xs, bc, dt, dtb_row, alog_row, ssd_expand_matrix(True), zxbc, y_fwd, dskip, nw)


def ssd_expand_matrix(reverse):
    rows = jnp.arange(LANES)[:, None]
    head = jnp.arange(SSD_INNER)[None, :] // HEAD_DIM
    return (rows == head + (SSD_HEADS if reverse else 0)).astype(BF16)


def ssd_mixer(zxbc, dt, conv_w, conv_b, dtb_row, alog_row, dskip, nw):
    y_f, xs, bc = _ssd_forward(zxbc, dt, conv_w, conv_b, dtb_row, alog_row)
    return _ssd_backward(zxbc, xs, bc, dt, dtb_row, alog_row, y_f, dskip, nw)


def _merge_kernel(x_ref, nw_ref, wg_ref, ona_ref, oswa_ref, ossd_ref, wna_ref, wswa_ref, wssd_ref, wout_ref,
                  nffn_ref, rhi_ref, rlo_ref, xo_ref, aff_ref):
    x = x_ref[...]
    h = _rmsnorm(x, nw_ref[...]).astype(BF16)
    d = x.shape[1]
    gates = jax.nn.sigmoid(_dot(h, wg_ref[...]))
    merged = (gates[:, :d] * _dot(ona_ref[...], wna_ref[...])
              + gates[:, d:2 * d] * _dot(oswa_ref[...], wswa_ref[...])
              + gates[:, 2 * d:] * _dot(ossd_ref[...], wssd_ref[...]))
    xn = x + _dot(merged.astype(BF16), wout_ref[...])
    xo_ref[...] = xn
    h2 = _rmsnorm(xn, nffn_ref[...])
    h_hi, h_lo = _split2(h2)
    logits = _dot(h_hi, rhi_ref[...]) + _dot(h_hi, rlo_ref[...]) + _dot(h_lo, rhi_ref[...])
    live = lax.broadcasted_iota(I32, logits.shape, 1) < N_EXPERTS
    logits = jnp.where(live, logits, NEG_INF)
    e = jnp.exp(logits - jnp.max(logits, axis=-1, keepdims=True))
    aff_ref[...] = e / jnp.sum(e, axis=-1, keepdims=True)


def merge_and_route(x2d, nw, wg, o_na, o_swa, o_ssd, w_na, w_swa, w_ssd, w_out, nffn, r_hi, r_lo, tm=256):
    t, d = x2d.shape
    tm = min(tm, t)
    row = lambda i: (i, 0)
    fixed = lambda i: (0, 0)
    return pl.pallas_call(
        _merge_kernel,
        grid=(t // tm,),
        in_specs=[pl.BlockSpec((tm, d), row), pl.BlockSpec((1, d), fixed), pl.BlockSpec(wg.shape, fixed),
                  pl.BlockSpec((tm, d), row), pl.BlockSpec((tm, d), row), pl.BlockSpec((tm, d), row),
                  pl.BlockSpec((d, d), fixed), pl.BlockSpec((d, d), fixed), pl.BlockSpec((d, d), fixed),
                  pl.BlockSpec((d, d), fixed), pl.BlockSpec((1, d), fixed),
                  pl.BlockSpec((d, LANES), fixed), pl.BlockSpec((d, LANES), fixed)],
        out_specs=[pl.BlockSpec((tm, d), row), pl.BlockSpec((tm, LANES), row)],
        out_shape=[jax.ShapeDtypeStruct((t, d), F32), jax.ShapeDtypeStruct((t, LANES), F32)],
        compiler_params=_cparams(("parallel",)),
        name="merge_and_route",
    )(x2d, nw, wg, o_na, o_swa, o_ssd, w_na, w_swa, w_ssd, w_out, nffn, r_hi, r_lo)


def _prefix_rows(mask_f, upper, lower_strict):
    lp = _dot(mask_f.astype(BF16), upper)
    rc = jnp.broadcast_to(lp[:, LANES - 1:LANES], lp.shape)
    return lp, rc, _dot(lower_strict, rc.astype(BF16))


def _select_kernel(a_ref, idx_ref, wts_ref, slot_ref, rstart_ref, *, cap, chunk):
    a = a_ref[0]
    nr = a.shape[0]
    bits = pltpu.bitcast(a, I32)

    def search(i, v):
        cand = v | lax.shift_left(jnp.int32(1), 30 - i)
        cnt = jnp.sum((bits >= cand).astype(F32))
        return jnp.where(cnt >= cap, cand, v)

    thr = lax.fori_loop(0, 31, search, jnp.int32(0))
    gt = bits > thr
    eq = bits == thr
    need = cap - jnp.sum(gt.astype(F32))

    li = lax.broadcasted_iota(I32, (LANES, LANES), 0)
    lj = lax.broadcasted_iota(I32, (LANES, LANES), 1)
    upper = (li <= lj).astype(BF16)
    ri = lax.broadcasted_iota(I32, (nr, nr), 0)
    rj = lax.broadcasted_iota(I32, (nr, nr), 1)
    lower_strict = (rj < ri).astype(BF16)

    elp, _, ers = _prefix_rows(eq.astype(F32), upper, lower_strict)
    sel = gt | (eq & (elp + ers <= need))
    lp, rc, rs = _prefix_rows(sel.astype(F32), upper, lower_strict)
    slot_ref[0] = jnp.where(sel, lp + rs - 1.0, -1.0).astype(I32)
    rstart_ref[0] = rs.astype(I32)

    row_end = rs[:, 0:1] + rc[:, 0:1]
    rcount = rc[:, 0:1]
    lp_b = lp.astype(BF16)
    a1, a2, a3 = _split3(a)
    sub_r = lax.broadcasted_iota(I32, (nr, chunk), 0).astype(F32)
    sub_l = lax.broadcasted_iota(I32, (LANES, chunk), 0).astype(F32)
    for ch in range(cap // chunk):
        s = (ch * chunk + lax.broadcasted_iota(I32, (1, chunk), 1)).astype(F32)
        before = (row_end <= s).astype(F32)
        row_of = jnp.sum(before, axis=0, keepdims=True)
        j = s - jnp.sum(before * rcount, axis=0, keepdims=True)
        onehot = (sub_r == row_of).astype(BF16)
        g = _dot_tn(lp_b, onehot)
        lane_of = jnp.sum((g <= j).astype(F32), axis=0, keepdims=True)
        idx_ref[0, :, ch * chunk:(ch + 1) * chunk] = (row_of * LANES + lane_of).astype(I32)
        ga = _dot_tn(a1, onehot) + _dot_tn(a2, onehot) + _dot_tn(a3, onehot)
        wts_ref[0, :, ch * chunk:(ch + 1) * chunk] = jnp.sum(jnp.where(sub_l == lane_of, ga, 0.0), axis=0,
                                                               keepdims=True)


def select_tokens(aff_t, cap):
    ne, nr, _ = aff_t.shape
    chunk = min(1024, cap)
    blk = lambda e: (e, 0, 0)
    return pl.pallas_call(
        functools.partial(_select_kernel, cap=cap, chunk=chunk),
        grid=(ne,),
        in_specs=[pl.BlockSpec((1, nr, LANES), blk)],
        out_specs=[pl.BlockSpec((1, 1, cap), blk), pl.BlockSpec((1, 1, cap), blk),
                   pl.BlockSpec((1, nr, LANES), blk), pl.BlockSpec((1, nr, LANES), blk)],
        out_shape=[jax.ShapeDtypeStruct((ne, 1, cap), I32), jax.ShapeDtypeStruct((ne, 1, cap), F32),
                   jax.ShapeDtypeStruct((ne, nr, LANES), I32), jax.ShapeDtypeStruct((ne, nr, LANES), I32)],
        compiler_params=_cparams(("parallel",)),
        name="select_tokens",
    )(aff_t)


def _expert_kernel(idx_hbm, x_hbm, w_ref, nffn_ref, wg_ref, wu_ref, wd_ref, o_ref, idx_smem, xbuf0, xbuf1, isem,
                   gsem, *, tm):
    nt = pl.num_programs(1)
    n = pl.program_id(0) * nt + pl.program_id(1)
    total = pl.num_programs(0) * nt
    xbufs = (xbuf0, xbuf1)

    def idx_copy(step, s):
        return pltpu.make_async_copy(idx_hbm.at[step], idx_smem.at[s], isem.at[s])

    def gather_wait(s):
        pltpu.make_async_copy(x_hbm.at[pl.ds(0, tm), :], xbufs[s], gsem.at[s]).wait()

    def row_copy(s, r):
        tok = idx_smem[s, r]
        return pltpu.make_async_copy(x_hbm.at[pl.ds(tok, 1), :], xbufs[s].at[pl.ds(r, 1), :], gsem.at[s])

    @pl.when(n == 0)
    def _():
        idx_copy(0, 0).start()
        idx_copy(0, 0).wait()

        def body(r, carry):
            row_copy(0, r).start()
            return carry
        lax.fori_loop(0, tm, body, 0)
        idx_copy(jnp.minimum(1, total - 1), 1).start()

    def step(s):
        o = 1 - s
        gather_wait(s)
        idx_copy(jnp.minimum(n + 1, total - 1), o).wait()
        for r in range(tm):
            row_copy(o, r).start()
        idx_copy(jnp.minimum(n + 2, total - 1), s).start()
        h = _rmsnorm(xbufs[s][...], nffn_ref[...]).astype(BF16)
        gate = _dot(h, wg_ref[0])
        up = _dot(h, wu_ref[0])
        act = (gate * jax.nn.sigmoid(gate) * up).astype(BF16)
        y = _dot(act, wd_ref[0])
        eye = lax.broadcasted_iota(I32, (LANES, LANES), 0) == lax.broadcasted_iota(I32, (LANES, LANES), 1)
        wcol = jnp.concatenate(
            [jnp.sum(jnp.where(eye, jnp.broadcast_to(w_ref[0, r:r + 1, :], (LANES, LANES)), 0.0), axis=1,
                     keepdims=True) for r in range(tm // LANES)], axis=0)
        o_ref[...] = (y * wcol).astype(o_ref.dtype)

        @pl.when(n == total - 1)
        def _():
            gather_wait(o)
            idx_copy(0, s).wait()

    for s in range(2):
        pl.when(n % 2 == s)(functools.partial(step, s))


def run_experts(idx_tiles, x2d, wts_rows, nffn, wg, wu, wd, cap, tm):
    ne = wg.shape[0]
    t, d = x2d.shape
    nt = cap // tm
    ff = wg.shape[2]
    return pl.pallas_call(
        functools.partial(_expert_kernel, tm=tm),
        grid=(ne, nt),
        in_specs=[pl.BlockSpec(memory_space=pl.ANY), pl.BlockSpec(memory_space=pl.ANY),
                  pl.BlockSpec((1, tm // LANES, LANES), lambda e, i: (e * nt + i, 0, 0)),
                  pl.BlockSpec((1, d), lambda e, i: (0, 0)),
                  pl.BlockSpec((1, d, ff), lambda e, i: (e, 0, 0)),
                  pl.BlockSpec((1, d, ff), lambda e, i: (e, 0, 0)),
                  pl.BlockSpec((1, ff, d), lambda e, i: (e, 0, 0))],
        out_specs=pl.BlockSpec((tm, d), lambda e, i: (e * nt + i, 0)),
        out_shape=jax.ShapeDtypeStruct((ne * cap, d), BF16),
        scratch_shapes=[pltpu.SMEM((2, tm), I32), pltpu.VMEM((tm, d), F32), pltpu.VMEM((tm, d), F32),
                        pltpu.SemaphoreType.DMA((2,)), pltpu.SemaphoreType.DMA((2,))],
        compiler_params=_cparams(("arbitrary", "arbitrary")),
        name="run_experts",
    )(idx_tiles, x2d, wts_rows, nffn, wg, wu, wd)


COMB_ALIGN = 16
COMB_SMALL = 64
COMB_BIG = LANES + COMB_ALIGN


def _combine_kernel(rstart_ref, x_ref, slot_ref, ye_hbm, o_ref, sbuf, bbuf, ssem, bsem, *, cap):
    r = pl.program_id(0)
    nr = pl.num_programs(0)
    ne = N_EXPERTS
    cur = r % 2

    def start_of(row, e, win):
        p = e * cap + rstart_ref[row * ne + e]
        return jnp.minimum((p // COMB_ALIGN) * COMB_ALIGN, ne * cap - win)

    def fits_small(row):
        ok = None
        for e in range(ne):
            end = e * cap + rstart_ref[(row + 1) * ne + e]
            fit = end - start_of(row, e, COMB_SMALL) <= COMB_SMALL
            ok = fit if ok is None else (ok & fit)
        return ok

    def small_copy(row, e, s):
        st = pl.multiple_of(start_of(row, e, COMB_SMALL), COMB_ALIGN)
        return pltpu.make_async_copy(ye_hbm.at[pl.ds(st, COMB_SMALL), :],
                                     sbuf.at[s, pl.ds(e * COMB_SMALL, COMB_SMALL), :], ssem.at[s, e])

    def big_copy(row, e, s):
        st = pl.multiple_of(start_of(row, e, COMB_BIG), COMB_ALIGN)
        return pltpu.make_async_copy(ye_hbm.at[pl.ds(st, COMB_BIG), :], bbuf.at[s, e], bsem.at[s, e])

    def issue(row, s):
        small = fits_small(row)

        @pl.when(small)
        def _():
            for e in range(ne):
                small_copy(row, e, s).start()

        @pl.when(jnp.logical_not(small))
        def _():
            for e in range(ne):
                big_copy(row, e, s).start()

    @pl.when(r == 0)
    def _():
        issue(0, 0)

    @pl.when(r + 1 < nr)
    def _():
        issue(r + 1, 1 - cur)

    small = fits_small(r)
    slots = slot_ref[...]

    def rel(e, win, shift):
        sl = slots[:, e:e + 1]
        return jnp.where(sl >= 0, sl + (e * cap + shift - start_of(r, e, win)), -1)

    @pl.when(small)
    def _():
        for e in range(ne):
            small_copy(r, e, cur).wait()
        lane = lax.broadcasted_iota(I32, (LANES, LANES), 1)
        pieces = [(lane == jnp.where(lane < COMB_SMALL, rel(2 * k, COMB_SMALL, 0),
                                     rel(2 * k + 1, COMB_SMALL, COMB_SMALL))).astype(BF16)
                  for k in range(ne // 2)]
        o_ref[...] = x_ref[...] + _dot(jnp.concatenate(pieces, axis=1), sbuf[cur])

    @pl.when(jnp.logical_not(small))
    def _():
        acc = x_ref[...]
        lane = lax.broadcasted_iota(I32, (LANES, COMB_BIG), 1)
        for e in range(ne):
            big_copy(r, e, cur).wait()
            acc = acc + _dot((lane == rel(e, COMB_BIG, 0)).astype(BF16), bbuf[cur, e])
        o_ref[...] = acc


def combine(x2d, slot_t, rstart_flat, ye, cap):
    t, d = x2d.shape
    nr = t // LANES
    ne = N_EXPERTS
    grid_spec = pltpu.PrefetchScalarGridSpec(
        num_scalar_prefetch=1,
        grid=(nr,),
        in_specs=[pl.BlockSpec((LANES, d), lambda r, rs: (r, 0)),
                  pl.BlockSpec((LANES, ne), lambda r, rs: (r, 0)),
                  pl.BlockSpec(memory_space=pl.ANY)],
        out_specs=pl.BlockSpec((LANES, d), lambda r, rs: (r, 0)),
        scratch_shapes=[pltpu.VMEM((2, ne * COMB_SMALL, d), BF16), pltpu.VMEM((2, ne, COMB_BIG, d), BF16),
                        pltpu.SemaphoreType.DMA((2, ne)), pltpu.SemaphoreType.DMA((2, ne))],
    )
    return pl.pallas_call(
        functools.partial(_combine_kernel, cap=cap),
        grid_spec=grid_spec,
        out_shape=jax.ShapeDtypeStruct((t, d), F32),
        compiler_params=_cparams(("arbitrary",)),
        name="combine",
    )(rstart_flat, x2d, slot_t, ye)


def moe_layer(x2d, aff, nffn, wg, wu, wd):
    t, d = x2d.shape
    ne = N_EXPERTS
    cap = CAPACITY_FACTOR * t // ne
    nr = t // LANES
    tm = min(512, cap)
    aff_t = aff[:, :ne].T.reshape(ne, nr, LANES)
    idx, wts, slot, rstart = select_tokens(aff_t, cap)
    ye = run_experts(idx.reshape(ne * cap // tm, tm), x2d, wts.reshape(ne * cap // tm, tm // LANES, LANES), nffn,
                     wg, wu, wd, cap, tm)
    slot_t = slot.reshape(ne, t).T
    rstart_flat = jnp.concatenate([rstart[:, :, 0].T, jnp.full((1, ne), cap, I32)], axis=0).reshape((nr + 1) * ne)
    return combine(x2d, slot_t, rstart_flat, ye, cap)


def _prep_layer(p, l):
    w_in = p["w_in"][l]
    d = D_MODEL
    c_na, c_swa = 3 * d, d + 2 * SWA_KV_HEADS * HEAD_DIM
    o_ssd = c_na + c_swa
    c_ssd = d + (d + 2 * SSD_GROUPS * SSD_STATE)
    o_dt = o_ssd + c_ssd
    o_gate = o_dt + 2 * SSD_HEADS
    w_dt = jnp.pad(w_in[:, o_dt:o_gate], ((0, 0), (0, LANES - 2 * SSD_HEADS)))
    pad_row = lambda v: jnp.pad(v.astype(F32).reshape(-1), (0, LANES - 2 * SSD_HEADS))[None, :]
    r_hi, r_lo = _split2(jnp.pad(p["router_w"][l], ((0, 0), (0, LANES - N_EXPERTS))))
    scale = HEAD_DIM ** -0.5
    head_row = lambda qn, kn, nq, nk: jnp.concatenate(
        [jnp.tile(qn.astype(F32) * scale, nq), jnp.tile(kn.astype(F32), nk)])[None, :]
    w_swa = w_in[:, c_na:o_ssd]
    dup = lambda w: jnp.repeat(w.reshape(d, SWA_KV_HEADS, 1, HEAD_DIM), 2, axis=2).reshape(d, 2 * w.shape[1])
    kv_w = SWA_KV_HEADS * HEAD_DIM
    w_swa = jnp.concatenate([w_swa[:, :d], dup(w_swa[:, d:d + kv_w]), dup(w_swa[:, d + kv_w:])], axis=1)
    return dict(
        norm_mix=p["norm_mix"][l][None, :],
        w_na_in=w_in[:, :c_na].astype(BF16),
        w_swa_in=w_swa.astype(BF16),
        w_ssd_in=w_in[:, o_ssd:o_dt].astype(BF16),
        w_dt=w_dt.astype(BF16),
        w_gate=w_in[:, o_gate:].astype(BF16),
        na_heads=head_row(p["na_q_norm"][l], p["na_k_norm"][l], NA_HEADS, NA_HEADS),
        na_tab=na_bias_table(p["na_rel_bias"][l]),
        swa_heads=head_row(p["swa_q_norm"][l], p["swa_k_norm"][l], SWA_Q_HEADS, 2 * SWA_KV_HEADS),
        swa_sink=p["swa_sink"][l].astype(F32),
        conv_w=p["ssd_conv_w"][l].astype(F32), conv_b=p["ssd_conv_b"][l].astype(F32)[None, :],
        dtb_row=pad_row(p["ssd_dt_bias"][l]), alog_row=pad_row(p["ssd_a_log"][l]),
        dskip=jnp.repeat(p["ssd_d"][l].astype(F32), HEAD_DIM)[None, :],
        ssd_norm=p["ssd_norm"][l].astype(F32)[None, :],
        w_na=p["w_branch_na"][l].astype(BF16), w_swa=p["w_branch_swa"][l].astype(BF16),
        w_ssd=p["w_branch_ssd"][l].astype(BF16), w_out=p["w_out"][l].astype(BF16),
        norm_ffn=p["norm_ffn"][l][None, :], r_hi=r_hi, r_lo=r_lo,
        wg=p["expert_w_gate"][l].astype(BF16), wu=p["expert_w_up"][l].astype(BF16),
        wd=p["expert_w_down"][l].astype(BF16),
    )


def _layer(x, lp, cos, sin):
    bsz, seq, d = x.shape
    t = bsz * seq
    x2d = x.reshape(t, d)
    qkv_na = norm_proj_heads(x2d, lp["norm_mix"], lp["w_na_in"], lp["na_heads"], seq)
    qkv_swa = norm_proj_heads(x2d, lp["norm_mix"], lp["w_swa_in"], lp["swa_heads"], seq, cos, sin)
    zxbc, dt = norm_proj(x2d, lp["norm_mix"], [lp["w_ssd_in"], lp["w_dt"]], [BF16, F32])
    o_na = neighborhood_attention(qkv_na.reshape(bsz, seq, -1), lp["na_tab"])
    o_swa = sliding_window_attention(qkv_swa.reshape(bsz, seq, -1), lp["swa_sink"])
    o_ssd = ssd_mixer(zxbc.reshape(bsz, seq, -1), dt.reshape(bsz, seq, -1), lp["conv_w"], lp["conv_b"],
                      lp["dtb_row"], lp["alog_row"], lp["dskip"], lp["ssd_norm"])
    x_mid, aff = merge_and_route(x2d, lp["norm_mix"], lp["w_gate"], o_na.reshape(t, d), o_swa.reshape(t, d),
                                 o_ssd.reshape(t, d), lp["w_na"], lp["w_swa"], lp["w_ssd"], lp["w_out"],
                                 lp["norm_ffn"], lp["r_hi"], lp["r_lo"])
    return moe_layer(x_mid, aff, lp["norm_ffn"], lp["wg"], lp["wu"], lp["wd"]).reshape(bsz, seq, d)


def kernel(x_prompt, x_sample, norm_mix, w_in, na_q_norm, na_k_norm, na_rel_bias, swa_q_norm, swa_k_norm, swa_sink, ssd_conv_w, ssd_conv_b, ssd_dt_bias, ssd_a_log, ssd_d, ssd_norm, w_branch_na, w_branch_swa, w_branch_ssd, w_out, norm_ffn, router_w, expert_w_gate, expert_w_up, expert_w_down):
    p = dict(norm_mix=norm_mix, w_in=w_in, na_q_norm=na_q_norm, na_k_norm=na_k_norm, na_rel_bias=na_rel_bias,
             swa_q_norm=swa_q_norm, swa_k_norm=swa_k_norm, swa_sink=swa_sink, ssd_conv_w=ssd_conv_w,
             ssd_conv_b=ssd_conv_b, ssd_dt_bias=ssd_dt_bias, ssd_a_log=ssd_a_log, ssd_d=ssd_d, ssd_norm=ssd_norm,
             w_branch_na=w_branch_na, w_branch_swa=w_branch_swa, w_branch_ssd=w_branch_ssd, w_out=w_out,
             norm_ffn=norm_ffn, router_w=router_w, expert_w_gate=expert_w_gate, expert_w_up=expert_w_up,
             expert_w_down=expert_w_down)
    depth = w_in.shape[0]
    layers = [_prep_layer(p, l) for l in range(depth)]
    outs = []
    for x in (x_prompt, x_sample):
        cos, sin = rope_tables(x.shape[1])
        for lp in layers:
            x = _layer(x, lp, cos, sin)
        outs.append(x)
    return tuple(outs)
```

```python
import functools
import math

import jax
import jax.numpy as jnp
from jax import lax
from jax.experimental import pallas as pl
from jax.experimental.pallas import tpu as pltpu

F32 = jnp.float32
BF16 = jnp.bfloat16
I32 = jnp.int32

D_MODEL = 1024
HEAD_DIM = 64
GRID_W = 64
NA_HEADS = 16
NA_WIN_ROWS = 8
NA_WIN_COLS = 16
NA_QROWS = 4
NA_KROWS = 12
SWA_Q_HEADS = 16
SWA_KV_HEADS = 4
SWA_WINDOW = 128
SWA_BLOCK = 128
ROPE_THETA = 10000.0
SSD_HEADS = 16
SSD_GROUPS = 4
SSD_STATE = 128
SSD_CONV = 4
SSD_CHUNK = 128
SSD_INNER = 1024
N_EXPERTS = 16
EXPERT_FF = 2048
CAPACITY_FACTOR = 2
NORM_EPS = 1e-6
NEG_INF = -1e30
LOG2E = math.log2(math.e)
LANES = 128
VMEM_LIMIT = 56 * 1024 * 1024


def _cparams(sem):
    return pltpu.CompilerParams(dimension_semantics=sem, vmem_limit_bytes=VMEM_LIMIT)


def _rmsnorm(x, w):
    return x * lax.rsqrt(jnp.mean(x * x, axis=-1, keepdims=True) + NORM_EPS) * w


def _lo_mask(shape):
    return lax.broadcasted_iota(I32, shape, len(shape) - 1) < HEAD_DIM


def _headnorm2(x, w):
    lo = _lo_mask(x.shape)
    x2 = x * x
    sa = jnp.sum(jnp.where(lo, x2, 0.0), axis=-1, keepdims=True)
    sb = jnp.sum(jnp.where(lo, 0.0, x2), axis=-1, keepdims=True)
    ms = jnp.where(lo, sa, sb) * (1.0 / HEAD_DIM)
    return x * lax.rsqrt(ms + NORM_EPS) * w


def _stack_halves(x):
    lo = _lo_mask(x.shape)
    zero = jnp.zeros_like(x)
    return jnp.concatenate([jnp.where(lo, x, zero), jnp.where(lo, zero, x)], axis=0)


def _dot(a, b):
    return jnp.dot(a, b, preferred_element_type=F32)


def _dot_nt(a, b):
    return lax.dot_general(a, b, (((1,), (1,)), ((), ())), preferred_element_type=F32)


def _dot_tn(a, b):
    return lax.dot_general(a, b, (((0,), (0,)), ((), ())), preferred_element_type=F32)


def _split2(x):
    hi = x.astype(BF16)
    lo = (x - hi.astype(F32)).astype(BF16)
    return hi, lo


def _split3(x):
    a = x.astype(BF16)
    r = x - a.astype(F32)
    b = r.astype(BF16)
    c = (r - b.astype(F32)).astype(BF16)
    return a, b, c


PROJ_CHUNK = 512
SSD_HALO = 16


def _norm_proj_ssd_kernel(x_ref, xp_ref, xn_ref, nw_ref, wz_ref, wxbc_ref, wdt_ref, cw_ref, cb_ref,
                          o_ref, dt_ref, *, per_seq):
    i = pl.program_id(0)
    tm = x_ref.shape[0]
    halo = xp_ref.shape[0]
    rows = tm + 2 * halo
    xcat = jnp.concatenate([xp_ref[...], x_ref[...], xn_ref[...]], axis=0)
    h = _rmsnorm(xcat, nw_ref[...]).astype(BF16)
    h_main = h[halo:halo + tm]
    nz = wz_ref.shape[1]
    for c0 in range(0, nz, PROJ_CHUNK):
        o_ref[:, c0:c0 + PROJ_CHUNK] = _dot(h_main, wz_ref[:, c0:c0 + PROJ_CHUNK]).astype(o_ref.dtype)
    dt_ref[...] = _dot(h_main, wdt_ref[...])
    r = lax.broadcasted_iota(I32, (rows, 1), 0)
    has_prev = (i % per_seq != 0).astype(F32)
    has_next = (i % per_seq != per_seq - 1).astype(F32)
    inside = jnp.where(r < halo, has_prev, jnp.where(r >= halo + tm, has_next, 1.0))
    for c0 in range(0, wxbc_ref.shape[1], PROJ_CHUNK):
        cols = slice(c0, c0 + PROJ_CHUNK)
        xe = _dot(h, wxbc_ref[:, cols]) * inside
        w = cw_ref[:, cols]
        y = (w[0:1] * pltpu.roll(xe, 2, axis=0) + w[1:2] * pltpu.roll(xe, 1, axis=0) + w[2:3] * xe
             + w[3:4] * pltpu.roll(xe, rows - 1, axis=0) + cb_ref[:, cols])[halo:halo + tm]
        o_ref[:, nz + c0:nz + c0 + PROJ_CHUNK] = (y * jax.nn.sigmoid(y)).astype(o_ref.dtype)


def norm_proj_ssd(x2d, nw, w_z, w_xbc, w_dt, conv_w, conv_b, seq, tm=512):
    t, d = x2d.shape
    tm = min(tm, t, seq)
    hb = tm // SSD_HALO
    n = w_z.shape[1] + w_xbc.shape[1]
    fixed = lambda i: (0, 0)
    return pl.pallas_call(
        functools.partial(_norm_proj_ssd_kernel, per_seq=seq // tm),
        grid=(t // tm,),
        in_specs=[pl.BlockSpec((tm, d), lambda i: (i, 0)),
                  pl.BlockSpec((SSD_HALO, d), lambda i: (jnp.maximum(i * hb - 1, 0), 0)),
                  pl.BlockSpec((SSD_HALO, d), lambda i: (jnp.minimum(i * hb + hb, t // SSD_HALO - 1), 0)),
                  pl.BlockSpec((1, d), fixed), pl.BlockSpec(w_z.shape, fixed), pl.BlockSpec(w_xbc.shape, fixed),
                  pl.BlockSpec(w_dt.shape, fixed), pl.BlockSpec(conv_w.shape, fixed),
                  pl.BlockSpec(conv_b.shape, fixed)],
        out_specs=[pl.BlockSpec((tm, n), lambda i: (i, 0)), pl.BlockSpec((tm, LANES), lambda i: (i, 0))],
        out_shape=[jax.ShapeDtypeStruct((t, n), BF16), jax.ShapeDtypeStruct((t, LANES), F32)],
        compiler_params=_cparams(("parallel",)),
        name="norm_proj_ssd",
    )(x2d, x2d, x2d, nw, w_z, w_xbc, w_dt, conv_w, conv_b)


def _rope(x, cos, sin_signed):
    first = (lax.broadcasted_iota(I32, x.shape, 1) % HEAD_DIM) < HEAD_DIM // 2
    partner = jnp.where(first, pltpu.roll(x, LANES - HEAD_DIM // 2, axis=1), pltpu.roll(x, HEAD_DIM // 2, axis=1))
    return x * cos + partner * sin_signed


def _norm_proj_heads_kernel(*refs, n_norm, rope):
    if rope:
        x_ref, nw_ref, w_ref, hw_ref, cos_ref, sin_ref, o_ref = refs
    else:
        x_ref, nw_ref, w_ref, hw_ref, o_ref = refs
    h = _rmsnorm(x_ref[...], nw_ref[...]).astype(BF16)
    ncols = w_ref.shape[1]
    for c0 in range(0, ncols, PROJ_CHUNK):
        acc = _dot(h, w_ref[:, c0:c0 + PROJ_CHUNK])
        parts = []
        for j in range(PROJ_CHUNK // LANES):
            col = c0 // LANES + j
            blk = acc[:, j * LANES:(j + 1) * LANES]
            if col < n_norm:
                blk = _headnorm2(blk, hw_ref[:, col * LANES:(col + 1) * LANES])
                if rope:
                    blk = _rope(blk, cos_ref[...], sin_ref[...])
            parts.append(blk)
        o_ref[:, c0:c0 + PROJ_CHUNK] = jnp.concatenate(parts, axis=1).astype(o_ref.dtype)


def norm_proj_heads(x2d, nw, w, head_w, seq, cos=None, sin=None, tm=512):
    t, d = x2d.shape
    tm = min(tm, t, seq)
    n = w.shape[1]
    n_norm = head_w.shape[1] // LANES
    rope = cos is not None
    in_specs = [pl.BlockSpec((tm, d), lambda i: (i, 0)), pl.BlockSpec((1, d), lambda i: (0, 0)),
                pl.BlockSpec(w.shape, lambda i: (0, 0)), pl.BlockSpec(head_w.shape, lambda i: (0, 0))]
    args = [x2d, nw, w, head_w]
    if rope:
        per_seq = seq // tm
        in_specs += [pl.BlockSpec((tm, LANES), lambda i: (i % per_seq, 0))] * 2
        args += [cos, sin]
    return pl.pallas_call(
        functools.partial(_norm_proj_heads_kernel, n_norm=n_norm, rope=rope),
        grid=(t // tm,),
        in_specs=in_specs,
        out_specs=pl.BlockSpec((tm, n), lambda i: (i, 0)),
        out_shape=jax.ShapeDtypeStruct((t, n), BF16),
        compiler_params=_cparams(("parallel",)),
        name="norm_proj_heads",
    )(*args)


NA_PAIRS = 4


def _na_kernel(q_ref, k0_ref, k1_ref, k2_ref, v0_ref, v1_ref, v2_ref, tab_ref, o_ref):
    g = pl.program_id(2)
    ng = pl.num_programs(2)
    variant = jnp.where(g == 0, 0, jnp.where(g == ng - 1, 2, 1))
    nk = NA_KROWS * GRID_W
    outs = []
    for hp in range(NA_PAIRS):
        cols = slice(hp * LANES, (hp + 1) * LANES)
        kw = jnp.concatenate([k0_ref[0, :, cols], k1_ref[0, :, cols], k2_ref[0, :, cols]], axis=0)
        s = _dot_nt(q_ref[0, :, cols], _stack_halves(kw)) + tab_ref[hp, variant]
        sa, sb = s[:, :nk], s[:, nk:]
        pa = jnp.exp2(sa - jnp.max(sa, axis=-1, keepdims=True))
        pb = jnp.exp2(sb - jnp.max(sb, axis=-1, keepdims=True))
        la = jnp.sum(pa, axis=-1, keepdims=True)
        lb = jnp.sum(pb, axis=-1, keepdims=True)
        p = jnp.concatenate([pa, pb], axis=1).astype(BF16)
        vw = jnp.concatenate([v0_ref[0, :, cols], v1_ref[0, :, cols], v2_ref[0, :, cols]], axis=0)
        o = _dot(p, _stack_halves(vw))
        outs.append(o * jnp.where(_lo_mask(o.shape), 1.0 / la, 1.0 / lb))
    o_ref[0] = jnp.concatenate(outs, axis=1).astype(o_ref.dtype)


def na_bias_table(rel_bias):
    nh = rel_bias.shape[0]
    offs = jnp.array([0, NA_QROWS, 2 * NA_QROWS], I32)
    r0w = jnp.array([[0, 0, 0, 0], [0, 1, 2, 3], [4, 4, 4, 4]], I32)
    j = jnp.arange(NA_QROWS, dtype=I32)
    i = jnp.arange(NA_KROWS, dtype=I32)
    c = jnp.arange(GRID_W, dtype=I32)
    qrow = offs[:, None] + j[None, :]
    row_rel = i[None, None, :] - qrow[:, :, None] + NA_WIN_ROWS - 1
    row_ok = (i[None, None, :] >= r0w[:, :, None]) & (i[None, None, :] < r0w[:, :, None] + NA_WIN_ROWS)
    c0 = jnp.clip(c - NA_WIN_COLS // 2, 0, GRID_W - NA_WIN_COLS)
    col_rel = c[None, :] - c[:, None] + NA_WIN_COLS - 1
    col_ok = (c[None, :] >= c0[:, None]) & (c[None, :] < c0[:, None] + NA_WIN_COLS)
    row_sel = (row_rel[..., None] == jnp.arange(2 * NA_WIN_ROWS - 1)).astype(F32)
    col_sel = (col_rel[..., None] == jnp.arange(2 * NA_WIN_COLS - 1)).astype(F32)
    by_col = jnp.einsum('hab,ckb->hack', rel_bias.astype(F32), col_sel, precision=lax.Precision.HIGHEST)
    b = jnp.einsum('vjia,hack->hvjcik', row_sel, by_col, precision=lax.Precision.HIGHEST)
    ok = row_ok[:, :, None, :, None] & col_ok[None, None, :, None, :]
    b = jnp.where(ok[None], b * LOG2E, NEG_INF)
    b = b.reshape(nh // 2, 2, 3, NA_QROWS * GRID_W, NA_KROWS * GRID_W)
    return jnp.concatenate([b[:, 0], b[:, 1]], axis=-1)


def neighborhood_attention(qkv, tab):
    bsz, seq, _ = qkv.shape
    rows = seq // GRID_W
    assert rows % NA_QROWS == 0 and rows >= NA_KROWS + NA_QROWS
    ng = rows // NA_QROWS
    tq = NA_QROWS * GRID_W
    wblk = NA_PAIRS * LANES
    nblk = NA_HEADS * HEAD_DIM // wblk

    def kv_spec(col0, j):
        return pl.BlockSpec((1, tq, wblk), lambda hp, b, g: (b, jnp.clip(g - 1, 0, ng - 3) + j, col0 + hp))

    return pl.pallas_call(
        _na_kernel,
        grid=(nblk, bsz, ng),
        in_specs=[pl.BlockSpec((1, tq, wblk), lambda hp, b, g: (b, g, hp))]
        + [kv_spec(nblk, j) for j in range(3)] + [kv_spec(2 * nblk, j) for j in range(3)]
        + [pl.BlockSpec((NA_PAIRS, 3, tq, 2 * NA_KROWS * GRID_W), lambda hp, b, g: (hp, 0, 0, 0),
                        pipeline_mode=pl.Buffered(1))],
        out_specs=pl.BlockSpec((1, tq, wblk), lambda hp, b, g: (b, g, hp)),
        out_shape=jax.ShapeDtypeStruct((bsz, seq, NA_HEADS * HEAD_DIM), BF16),
        compiler_params=_cparams(("parallel", "parallel", "parallel")),
        name="neighborhood_attention",
    )(qkv, qkv, qkv, qkv, qkv, qkv, qkv, tab)


def _swa_kernel(sink_ref, q_ref, kp_ref, kc_ref, kx_ref, vp_ref, vc_ref, vx_ref, o_ref, *, seq):
    i = pl.program_id(1)
    blk = SWA_BLOCK
    span = 3 * blk
    qpos = i * blk + lax.broadcasted_iota(I32, (2 * blk, span), 0) % blk
    kpos = (i - 1) * blk + lax.broadcasted_iota(I32, (2 * blk, span), 1)
    valid = (jnp.abs(kpos - qpos) <= SWA_WINDOW) & (kpos >= 0) & (kpos < seq)
    upper = lax.broadcasted_iota(I32, (2 * blk, 1), 0) < blk
    group = SWA_Q_HEADS // SWA_KV_HEADS
    qw = group * HEAD_DIM

    def softmax_part(sc, sink):
        sc = jnp.where(valid, sc, NEG_INF)
        m = jnp.maximum(jnp.max(sc, axis=-1, keepdims=True), sink)
        e = jnp.exp2(sc - m)
        return e, jnp.sum(e, axis=-1, keepdims=True) + jnp.exp2(sink - m)

    outs = []
    for kvh in range(SWA_KV_HEADS):
        q = q_ref[0, :, kvh * qw:(kvh + 1) * qw]
        q2 = jnp.concatenate([q[:, :LANES], q[:, LANES:]], axis=0)
        cols = slice(kvh * LANES, (kvh + 1) * LANES)
        kk = jnp.concatenate([kp_ref[0, :, cols], kc_ref[0, :, cols], kx_ref[0, :, cols]], axis=0)
        s = _dot_nt(q2, _stack_halves(kk))
        h0 = group * kvh
        sinks = [sink_ref[h0 + j] * LOG2E for j in range(group)]
        ea, la = softmax_part(s[:, :span], jnp.where(upper, sinks[0], sinks[2]))
        eb, lb = softmax_part(s[:, span:], jnp.where(upper, sinks[1], sinks[3]))
        p = jnp.concatenate([ea, eb], axis=1).astype(BF16)
        vv = jnp.concatenate([vp_ref[0, :, cols], vc_ref[0, :, cols], vx_ref[0, :, cols]], axis=0)
        o = _dot(p, _stack_halves(vv))
        o = o * jnp.where(_lo_mask(o.shape), 1.0 / la, 1.0 / lb)
        outs.append(jnp.concatenate([o[:blk], o[blk:]], axis=1))
    o_ref[0] = jnp.concatenate(outs, axis=1).astype(o_ref.dtype)


def rope_tables(seq):
    half = HEAD_DIM // 2
    inv = ROPE_THETA ** (-jnp.arange(half, dtype=F32) / half)
    ang = jnp.arange(seq, dtype=F32)[:, None] * inv[None, :]
    cos, sin = jnp.cos(ang), jnp.sin(ang)
    return jnp.tile(cos, (1, 4)), jnp.tile(jnp.concatenate([-sin, sin], axis=1), (1, 2))


def sliding_window_attention(qkv, sink):
    bsz, seq, _ = qkv.shape
    blk = SWA_BLOCK
    nblk = seq // blk
    qcols = SWA_Q_HEADS * HEAD_DIM
    kvw = SWA_KV_HEADS * LANES

    def kv_spec(col, j):
        return pl.BlockSpec((1, blk, kvw), lambda b, i: (b, jnp.clip(i + j - 1, 0, nblk - 1), col))

    return pl.pallas_call(
        functools.partial(_swa_kernel, seq=seq),
        grid=(bsz, nblk),
        in_specs=[pl.BlockSpec(memory_space=pltpu.SMEM), pl.BlockSpec((1, blk, qcols), lambda b, i: (b, i, 0))]
        + [kv_spec(qcols // kvw, j) for j in range(3)] + [kv_spec(qcols // kvw + 1, j) for j in range(3)],
        out_specs=pl.BlockSpec((1, blk, qcols), lambda b, i: (b, i, 0)),
        out_shape=jax.ShapeDtypeStruct((bsz, seq, qcols), BF16),
        compiler_params=_cparams(("parallel", "parallel")),
        name="sliding_window_attention",
    )(sink, qkv, qkv, qkv, qkv, qkv, qkv, qkv)


SSD_STEP_CHUNKS = 2


def _ssd_chunk(xs, bc, dt_raw, h, dtb, alog, e_mat, reverse):
    ln = SSD_CHUNK
    doff = SSD_HEADS if reverse else 0
    gw = SSD_INNER // SSD_GROUPS
    dt = jax.nn.softplus(dt_raw + dtb)
    a_dt = dt * (-jnp.exp(alog))
    tr = lax.broadcasted_iota(I32, (ln, ln), 0)
    tc = lax.broadcasted_iota(I32, (ln, ln), 1)
    scanned = (tc >= tr) if reverse else (tc <= tr)
    tri = scanned.astype(BF16)
    a1, a2, a3 = _split3(a_dt)
    cum = _dot(tri, a1) + _dot(tri, a2) + _dot(tri, a3)
    tot = jnp.sum(a_dt, axis=0, keepdims=True)
    cum2 = cum * LOG2E
    cum2_t = cum2.T

    stacked = jnp.concatenate([dt, jnp.exp(tot - cum), jnp.exp(cum), jnp.broadcast_to(jnp.exp(tot), (8, LANES))],
                              axis=0)
    s_hi, s_lo = _split2(stacked)
    expanded = _dot(s_hi, e_mat) + _dot(s_lo, e_mat)
    dt_e, wst_e, ecum_e, etot_e = (expanded[:ln], expanded[ln:2 * ln], expanded[2 * ln:3 * ln],
                                   expanded[3 * ln:3 * ln + 1])
    xdt = xs * dt_e
    xw = (xdt * wst_e).astype(BF16)
    xdt_b = xdt.astype(BF16)

    bm = bc[:, :SSD_GROUPS * SSD_STATE].astype(BF16)
    cm = bc[:, SSD_GROUPS * SSD_STATE:].astype(BF16)
    y_parts = []
    new_states = []
    for g in range(SSD_GROUPS):
        b_g = bm[:, g * SSD_STATE:(g + 1) * SSD_STATE]
        c_g = cm[:, g * SSD_STATE:(g + 1) * SSD_STATE]
        cb = _dot_nt(c_g, b_g)
        h_g = h[:, g * gw:(g + 1) * gw]
        y_off = _dot(c_g, h_g.astype(BF16)) * ecum_e[:, g * gw:(g + 1) * gw]
        for pair in range(2):
            ms = []
            for hh in range(2):
                lane = doff + 4 * g + 2 * pair + hh
                seg = cum2[:, lane:lane + 1] - cum2_t[lane:lane + 1, :]
                ms.append((cb * jnp.exp2(jnp.where(scanned, seg, NEG_INF))).astype(BF16))
            k = 2 * g + pair
            y_parts.append(_dot(jnp.concatenate(ms, axis=1), _stack_halves(xdt_b[:, k * LANES:(k + 1) * LANES]))
                           + y_off[:, pair * LANES:(pair + 1) * LANES])
        new_states.append(h_g * etot_e[:, g * gw:(g + 1) * gw] + _dot_tn(b_g, xw[:, g * gw:(g + 1) * gw]))
    return jnp.concatenate(y_parts, axis=1), jnp.concatenate(new_states, axis=1)


def _ssd_kernel(*refs, reverse):
    if reverse:
        (xs_ref, bc_ref, dt_ref, dtb_ref, alog_ref, e_ref, z_ref, yprev_ref, dskip_ref, nw_ref,
         o_ref, h_ref) = refs
    else:
        xs_ref, bc_ref, dt_ref, dtb_ref, alog_ref, e_ref, o_ref, h_ref = refs
    step = pl.program_id(1)
    ln = SSD_CHUNK
    gw = SSD_INNER // SSD_GROUPS

    @pl.when(step == 0)
    def _():
        h_ref[...] = jnp.zeros_like(h_ref)

    xs_all = xs_ref[0].astype(F32)
    bc_all = bc_ref[0].astype(F32)
    h = h_ref[...]
    ys = [None] * SSD_STEP_CHUNKS
    order = range(SSD_STEP_CHUNKS - 1, -1, -1) if reverse else range(SSD_STEP_CHUNKS)
    for ci in order:
        rows = slice(ci * ln, (ci + 1) * ln)
        ys[ci], h = _ssd_chunk(xs_all[rows], bc_all[rows], dt_ref[0, rows, :], h, dtb_ref[...], alog_ref[...],
                               e_ref[...], reverse)
    h_ref[...] = h
    y = jnp.concatenate(ys, axis=0)

    if reverse:
        y = y + yprev_ref[0].astype(F32) + dskip_ref[...] * xs_all
        z = z_ref[0].astype(F32)
        y = y * (z * jax.nn.sigmoid(z))
        outs = []
        for g in range(SSD_GROUPS):
            yg = y[:, g * gw:(g + 1) * gw]
            outs.append(yg * lax.rsqrt(jnp.mean(yg * yg, axis=-1, keepdims=True) + NORM_EPS))
        y = jnp.concatenate(outs, axis=1) * nw_ref[...]
    o_ref[0] = y.astype(o_ref.dtype)


def _ssd_pass(zxbc, dt, dtb_row, alog_row, reverse, final_args=()):
    bsz, seq, _ = zxbc.shape
    rows = SSD_STEP_CHUNKS * SSD_CHUNK
    nb = seq // rows

    def block(s):
        return (nb - 1 - s) if reverse else s

    def main_spec(col):
        return pl.BlockSpec((1, rows, SSD_INNER), lambda b, s: (b, block(s), col))

    def const_spec(shape):
        return pl.BlockSpec(shape, lambda b, s: (0, 0))

    in_specs = [main_spec(1), main_spec(2), pl.BlockSpec((1, rows, LANES), lambda b, s: (b, block(s), 0)),
                const_spec((1, LANES)), const_spec((1, LANES)), const_spec((LANES, SSD_INNER))]
    args = [zxbc, zxbc, dt, dtb_row, alog_row, ssd_expand_matrix(reverse)]
    if reverse:
        y_fwd, dskip, nw = final_args
        in_specs += [main_spec(0), main_spec(0), const_spec((1, SSD_INNER)), const_spec((1, SSD_INNER))]
        args += [zxbc, y_fwd, dskip, nw]
    return pl.pallas_call(
        functools.partial(_ssd_kernel, reverse=reverse),
        grid=(bsz, nb),
        in_specs=in_specs,
        out_specs=main_spec(0),
        out_shape=jax.ShapeDtypeStruct((bsz, seq, SSD_INNER), BF16),
        scratch_shapes=[pltpu.VMEM((SSD_STATE, SSD_INNER), F32)],
        compiler_params=_cparams(("parallel", "arbitrary")),
        name="ssd_bwd_final" if reverse else "ssd_fwd",
    )(*args)


def ssd_expand_matrix(reverse):
    rows = jnp.arange(LANES)[:, None]
    head = jnp.arange(SSD_INNER)[None, :] // HEAD_DIM
    return (rows == head + (SSD_HEADS if reverse else 0)).astype(BF16)


def ssd_mixer(zxbc, dt, dtb_row, alog_row, dskip, nw):
    y_f = _ssd_pass(zxbc, dt, dtb_row, alog_row, reverse=False)
    return _ssd_pass(zxbc, dt, dtb_row, alog_row, reverse=True, final_args=(y_f, dskip, nw))


def _merge_kernel(x_ref, nw_ref, wg_ref, ona_ref, oswa_ref, ossd_ref, wna_ref, wswa_ref, wssd_ref, wout_ref,
                  nffn_ref, rhi_ref, rlo_ref, xo_ref, aff_ref):
    x = x_ref[...]
    h = _rmsnorm(x, nw_ref[...]).astype(BF16)
    d = x.shape[1]
    gates = jax.nn.sigmoid(_dot(h, wg_ref[...]))
    merged = (gates[:, :d] * _dot(ona_ref[...], wna_ref[...])
              + gates[:, d:2 * d] * _dot(oswa_ref[...], wswa_ref[...])
              + gates[:, 2 * d:] * _dot(ossd_ref[...], wssd_ref[...]))
    xn = x + _dot(merged.astype(BF16), wout_ref[...])
    xo_ref[...] = xn
    h2 = _rmsnorm(xn, nffn_ref[...])
    h_hi, h_lo = _split2(h2)
    logits = _dot(h_hi, rhi_ref[...]) + _dot(h_hi, rlo_ref[...]) + _dot(h_lo, rhi_ref[...])
    live = lax.broadcasted_iota(I32, logits.shape, 1) < N_EXPERTS
    logits = jnp.where(live, logits, NEG_INF)
    e = jnp.exp(logits - jnp.max(logits, axis=-1, keepdims=True))
    aff_ref[...] = e / jnp.sum(e, axis=-1, keepdims=True)


def merge_and_route(x2d, nw, wg, o_na, o_swa, o_ssd, w_na, w_swa, w_ssd, w_out, nffn, r_hi, r_lo, tm=256):
    t, d = x2d.shape
    tm = min(tm, t)
    row = lambda i: (i, 0)
    fixed = lambda i: (0, 0)
    return pl.pallas_call(
        _merge_kernel,
        grid=(t // tm,),
        in_specs=[pl.BlockSpec((tm, d), row), pl.BlockSpec((1, d), fixed), pl.BlockSpec(wg.shape, fixed),
                  pl.BlockSpec((tm, d), row), pl.BlockSpec((tm, d), row), pl.BlockSpec((tm, d), row),
                  pl.BlockSpec((d, d), fixed), pl.BlockSpec((d, d), fixed), pl.BlockSpec((d, d), fixed),
                  pl.BlockSpec((d, d), fixed), pl.BlockSpec((1, d), fixed),
                  pl.BlockSpec((d, LANES), fixed), pl.BlockSpec((d, LANES), fixed)],
        out_specs=[pl.BlockSpec((tm, d), row), pl.BlockSpec((tm, LANES), row)],
        out_shape=[jax.ShapeDtypeStruct((t, d), F32), jax.ShapeDtypeStruct((t, LANES), F32)],
        compiler_params=_cparams(("parallel",)),
        name="merge_and_route",
    )(x2d, nw, wg, o_na, o_swa, o_ssd, w_na, w_swa, w_ssd, w_out, nffn, r_hi, r_lo)


def _prefix_rows(mask_f, upper, lower_strict):
    lp = _dot(mask_f.astype(BF16), upper)
    rc = jnp.broadcast_to(lp[:, LANES - 1:LANES], lp.shape)
    return lp, rc, _dot(lower_strict, rc.astype(BF16))


def _select_kernel(a_ref, idx_ref, wts_ref, slot_ref, rstart_ref, *, cap, chunk):
    a = a_ref[0]
    nr = a.shape[0]
    bits = pltpu.bitcast(a, I32)

    def search(i, v):
        cand = v | lax.shift_left(jnp.int32(1), 30 - i)
        cnt = jnp.sum((bits >= cand).astype(F32))
        return jnp.where(cnt >= cap, cand, v)

    thr = lax.fori_loop(0, 31, search, jnp.int32(0))
    gt = bits > thr
    eq = bits == thr
    need = cap - jnp.sum(gt.astype(F32))

    li = lax.broadcasted_iota(I32, (LANES, LANES), 0)
    lj = lax.broadcasted_iota(I32, (LANES, LANES), 1)
    upper = (li <= lj).astype(BF16)
    ri = lax.broadcasted_iota(I32, (nr, nr), 0)
    rj = lax.broadcasted_iota(I32, (nr, nr), 1)
    lower_strict = (rj < ri).astype(BF16)

    elp, _, ers = _prefix_rows(eq.astype(F32), upper, lower_strict)
    sel = gt | (eq & (elp + ers <= need))
    lp, rc, rs = _prefix_rows(sel.astype(F32), upper, lower_strict)
    slot_ref[0] = jnp.where(sel, lp + rs - 1.0, -1.0).astype(I32)
    rstart_ref[0] = rs.astype(I32)

    row_end = rs[:, 0:1] + rc[:, 0:1]
    rcount = rc[:, 0:1]
    lp_b = lp.astype(BF16)
    a1, a2, a3 = _split3(a)
    sub_r = lax.broadcasted_iota(I32, (nr, chunk), 0).astype(F32)
    sub_l = lax.broadcasted_iota(I32, (LANES, chunk), 0).astype(F32)
    for ch in range(cap // chunk):
        s = (ch * chunk + lax.broadcasted_iota(I32, (1, chunk), 1)).astype(F32)
        before = (row_end <= s).astype(F32)
        row_of = jnp.sum(before, axis=0, keepdims=True)
        j = s - jnp.sum(before * rcount, axis=0, keepdims=True)
        onehot = (sub_r == row_of).astype(BF16)
        g = _dot_tn(lp_b, onehot)
        lane_of = jnp.sum((g <= j).astype(F32), axis=0, keepdims=True)
        idx_ref[0, :, ch * chunk:(ch + 1) * chunk] = (row_of * LANES + lane_of).astype(I32)
        ga = _dot_tn(a1, onehot) + _dot_tn(a2, onehot) + _dot_tn(a3, onehot)
        wts_ref[0, :, ch * chunk:(ch + 1) * chunk] = jnp.sum(jnp.where(sub_l == lane_of, ga, 0.0), axis=0,
                                                               keepdims=True)


def select_tokens(aff_t, cap):
    ne, nr, _ = aff_t.shape
    chunk = min(1024, cap)
    blk = lambda e: (e, 0, 0)
    return pl.pallas_call(
        functools.partial(_select_kernel, cap=cap, chunk=chunk),
        grid=(ne,),
        in_specs=[pl.BlockSpec((1, nr, LANES), blk)],
        out_specs=[pl.BlockSpec((1, 1, cap), blk), pl.BlockSpec((1, 1, cap), blk),
                   pl.BlockSpec((1, nr, LANES), blk), pl.BlockSpec((1, nr, LANES), blk)],
        out_shape=[jax.ShapeDtypeStruct((ne, 1, cap), I32), jax.ShapeDtypeStruct((ne, 1, cap), F32),
                   jax.ShapeDtypeStruct((ne, nr, LANES), I32), jax.ShapeDtypeStruct((ne, nr, LANES), I32)],
        compiler_params=_cparams(("parallel",)),
        name="select_tokens",
    )(aff_t)


def _expert_kernel(idx_hbm, x_hbm, w_ref, nffn_ref, wg_ref, wu_ref, wd_ref, o_ref, idx_smem, xbuf0, xbuf1, isem,
                   gsem, *, tm):
    nt = pl.num_programs(1)
    n = pl.program_id(0) * nt + pl.program_id(1)
    total = pl.num_programs(0) * nt
    xbufs = (xbuf0, xbuf1)

    def idx_copy(step, s):
        return pltpu.make_async_copy(idx_hbm.at[step], idx_smem.at[s], isem.at[s])

    def gather_wait(s):
        pltpu.make_async_copy(x_hbm.at[pl.ds(0, tm), :], xbufs[s], gsem.at[s]).wait()

    def row_copy(s, r):
        tok = idx_smem[s, r]
        return pltpu.make_async_copy(x_hbm.at[pl.ds(tok, 1), :], xbufs[s].at[pl.ds(r, 1), :], gsem.at[s])

    @pl.when(n == 0)
    def _():
        idx_copy(0, 0).start()
        idx_copy(0, 0).wait()

        def body(r, carry):
            row_copy(0, r).start()
            return carry
        lax.fori_loop(0, tm, body, 0)
        idx_copy(jnp.minimum(1, total - 1), 1).start()

    def step(s):
        o = 1 - s
        gather_wait(s)
        idx_copy(jnp.minimum(n + 1, total - 1), o).wait()
        for r in range(tm):
            row_copy(o, r).start()
        idx_copy(jnp.minimum(n + 2, total - 1), s).start()
        h = _rmsnorm(xbufs[s][...], nffn_ref[...]).astype(BF16)
        gate = _dot(h, wg_ref[0])
        up = _dot(h, wu_ref[0])
        act = (gate * jax.nn.sigmoid(gate) * up).astype(BF16)
        y = _dot(act, wd_ref[0])
        eye = lax.broadcasted_iota(I32, (LANES, LANES), 0) == lax.broadcasted_iota(I32, (LANES, LANES), 1)
        wcol = jnp.concatenate(
            [jnp.sum(jnp.where(eye, jnp.broadcast_to(w_ref[0, r:r + 1, :], (LANES, LANES)), 0.0), axis=1,
                     keepdims=True) for r in range(tm // LANES)], axis=0)
        o_ref[...] = (y * wcol).astype(o_ref.dtype)

        @pl.when(n == total - 1)
        def _():
            gather_wait(o)
            idx_copy(0, s).wait()

    for s in range(2):
        pl.when(n % 2 == s)(functools.partial(step, s))


def run_experts(idx_tiles, x2d, wts_rows, nffn, wg, wu, wd, cap, tm):
    ne = wg.shape[0]
    t, d = x2d.shape
    nt = cap // tm
    ff = wg.shape[2]
    return pl.pallas_call(
        functools.partial(_expert_kernel, tm=tm),
        grid=(ne, nt),
        in_specs=[pl.BlockSpec(memory_space=pl.ANY), pl.BlockSpec(memory_space=pl.ANY),
                  pl.BlockSpec((1, tm // LANES, LANES), lambda e, i: (e * nt + i, 0, 0)),
                  pl.BlockSpec((1, d), lambda e, i: (0, 0)),
                  pl.BlockSpec((1, d, ff), lambda e, i: (e, 0, 0)),
                  pl.BlockSpec((1, d, ff), lambda e, i: (e, 0, 0)),
                  pl.BlockSpec((1, ff, d), lambda e, i: (e, 0, 0))],
        out_specs=pl.BlockSpec((tm, d), lambda e, i: (e * nt + i, 0)),
        out_shape=jax.ShapeDtypeStruct((ne * cap, d), BF16),
        scratch_shapes=[pltpu.SMEM((2, tm), I32), pltpu.VMEM((tm, d), F32), pltpu.VMEM((tm, d), F32),
                        pltpu.SemaphoreType.DMA((2,)), pltpu.SemaphoreType.DMA((2,))],
        compiler_params=_cparams(("arbitrary", "arbitrary")),
        name="run_experts",
    )(idx_tiles, x2d, wts_rows, nffn, wg, wu, wd)


COMB_ALIGN = 16
COMB_SMALL = 64
COMB_BIG = LANES + COMB_ALIGN


def _combine_kernel(rstart_ref, x_ref, slot_ref, ye_hbm, o_ref, sbuf, bbuf, ssem, bsem, *, cap):
    r = pl.program_id(0)
    nr = pl.num_programs(0)
    ne = N_EXPERTS
    cur = r % 2

    def start_of(row, e, win):
        p = e * cap + rstart_ref[row * ne + e]
        return jnp.minimum((p // COMB_ALIGN) * COMB_ALIGN, ne * cap - win)

    def fits_small(row):
        ok = None
        for e in range(ne):
            end = e * cap + rstart_ref[(row + 1) * ne + e]
            fit = end - start_of(row, e, COMB_SMALL) <= COMB_SMALL
            ok = fit if ok is None else (ok & fit)
        return ok

    def small_copy(row, e, s):
        st = pl.multiple_of(start_of(row, e, COMB_SMALL), COMB_ALIGN)
        return pltpu.make_async_copy(ye_hbm.at[pl.ds(st, COMB_SMALL), :],
                                     sbuf.at[s, pl.ds(e * COMB_SMALL, COMB_SMALL), :], ssem.at[s, e])

    def big_copy(row, e, s):
        st = pl.multiple_of(start_of(row, e, COMB_BIG), COMB_ALIGN)
        return pltpu.make_async_copy(ye_hbm.at[pl.ds(st, COMB_BIG), :], bbuf.at[s, e], bsem.at[s, e])

    def issue(row, s):
        small = fits_small(row)

        @pl.when(small)
        def _():
            for e in range(ne):
                small_copy(row, e, s).start()

        @pl.when(jnp.logical_not(small))
        def _():
            for e in range(ne):
                big_copy(row, e, s).start()

    @pl.when(r == 0)
    def _():
        issue(0, 0)

    @pl.when(r + 1 < nr)
    def _():
        issue(r + 1, 1 - cur)

    small = fits_small(r)
    slots = slot_ref[...]

    def rel(e, win, shift):
        sl = slots[:, e:e + 1]
        return jnp.where(sl >= 0, sl + (e * cap + shift - start_of(r, e, win)), -1)

    @pl.when(small)
    def _():
        for e in range(ne):
            small_copy(r, e, cur).wait()
        lane = lax.broadcasted_iota(I32, (LANES, LANES), 1)
        pieces = [(lane == jnp.where(lane < COMB_SMALL, rel(2 * k, COMB_SMALL, 0),
                                     rel(2 * k + 1, COMB_SMALL, COMB_SMALL))).astype(BF16)
                  for k in range(ne // 2)]
        o_ref[...] = x_ref[...] + _dot(jnp.concatenate(pieces, axis=1), sbuf[cur])

    @pl.when(jnp.logical_not(small))
    def _():
        acc = x_ref[...]
        lane = lax.broadcasted_iota(I32, (LANES, COMB_BIG), 1)
        for e in range(ne):
            big_copy(r, e, cur).wait()
            acc = acc + _dot((lane == rel(e, COMB_BIG, 0)).astype(BF16), bbuf[cur, e])
        o_ref[...] = acc


def combine(x2d, slot_t, rstart_flat, ye, cap):
    t, d = x2d.shape
    nr = t // LANES
    ne = N_EXPERTS
    grid_spec = pltpu.PrefetchScalarGridSpec(
        num_scalar_prefetch=1,
        grid=(nr,),
        in_specs=[pl.BlockSpec((LANES, d), lambda r, rs: (r, 0)),
                  pl.BlockSpec((LANES, ne), lambda r, rs: (r, 0)),
                  pl.BlockSpec(memory_space=pl.ANY)],
        out_specs=pl.BlockSpec((LANES, d), lambda r, rs: (r, 0)),
        scratch_shapes=[pltpu.VMEM((2, ne * COMB_SMALL, d), BF16), pltpu.VMEM((2, ne, COMB_BIG, d), BF16),
                        pltpu.SemaphoreType.DMA((2, ne)), pltpu.SemaphoreType.DMA((2, ne))],
    )
    return pl.pallas_call(
        functools.partial(_combine_kernel, cap=cap),
        grid_spec=grid_spec,
        out_shape=jax.ShapeDtypeStruct((t, d), F32),
        compiler_params=_cparams(("arbitrary",)),
        name="combine",
    )(rstart_flat, x2d, slot_t, ye)


def moe_layer(x2d, aff, nffn, wg, wu, wd):
    t, d = x2d.shape
    ne = N_EXPERTS
    cap = CAPACITY_FACTOR * t // ne
    nr = t // LANES
    tm = min(512, cap)
    aff_t = aff[:, :ne].T.reshape(ne, nr, LANES)
    idx, wts, slot, rstart = select_tokens(aff_t, cap)
    ye = run_experts(idx.reshape(ne * cap // tm, tm), x2d, wts.reshape(ne * cap // tm, tm // LANES, LANES), nffn,
                     wg, wu, wd, cap, tm)
    slot_t = slot.reshape(ne, t).T
    rstart_flat = jnp.concatenate([rstart[:, :, 0].T, jnp.full((1, ne), cap, I32)], axis=0).reshape((nr + 1) * ne)
    return combine(x2d, slot_t, rstart_flat, ye, cap)


def _prep_layer(p, l):
    w_in = p["w_in"][l]
    d = D_MODEL
    c_na, c_swa = 3 * d, d + 2 * SWA_KV_HEADS * HEAD_DIM
    o_ssd = c_na + c_swa
    c_ssd = d + (d + 2 * SSD_GROUPS * SSD_STATE)
    o_dt = o_ssd + c_ssd
    o_gate = o_dt + 2 * SSD_HEADS
    w_dt = jnp.pad(w_in[:, o_dt:o_gate], ((0, 0), (0, LANES - 2 * SSD_HEADS)))
    pad_row = lambda v: jnp.pad(v.astype(F32).reshape(-1), (0, LANES - 2 * SSD_HEADS))[None, :]
    r_hi, r_lo = _split2(jnp.pad(p["router_w"][l], ((0, 0), (0, LANES - N_EXPERTS))))
    scale = HEAD_DIM ** -0.5 * LOG2E
    head_row = lambda qn, kn, nq, nk: jnp.concatenate(
        [jnp.tile(qn.astype(F32) * scale, nq), jnp.tile(kn.astype(F32), nk)])[None, :]
    w_swa = w_in[:, c_na:o_ssd]
    dup = lambda w: jnp.repeat(w.reshape(d, SWA_KV_HEADS, 1, HEAD_DIM), 2, axis=2).reshape(d, 2 * w.shape[1])
    kv_w = SWA_KV_HEADS * HEAD_DIM
    w_swa = jnp.concatenate([w_swa[:, :d], dup(w_swa[:, d:d + kv_w]), dup(w_swa[:, d + kv_w:])], axis=1)
    return dict(
        norm_mix=p["norm_mix"][l][None, :],
        w_na_in=w_in[:, :c_na].astype(BF16),
        w_swa_in=w_swa.astype(BF16),
        w_ssd_z=w_in[:, o_ssd:o_ssd + d].astype(BF16), w_ssd_xbc=w_in[:, o_ssd + d:o_dt].astype(BF16),
        w_dt=w_dt.astype(BF16),
        w_gate=w_in[:, o_gate:].astype(BF16),
        na_heads=head_row(p["na_q_norm"][l], p["na_k_norm"][l], NA_HEADS, NA_HEADS),
        na_tab=na_bias_table(p["na_rel_bias"][l]),
        swa_heads=head_row(p["swa_q_norm"][l], p["swa_k_norm"][l], SWA_Q_HEADS, 2 * SWA_KV_HEADS),
        swa_sink=p["swa_sink"][l].astype(F32),
        conv_w=p["ssd_conv_w"][l].astype(F32), conv_b=p["ssd_conv_b"][l].astype(F32)[None, :],
        dtb_row=pad_row(p["ssd_dt_bias"][l]), alog_row=pad_row(p["ssd_a_log"][l]),
        dskip=jnp.repeat(p["ssd_d"][l].astype(F32), HEAD_DIM)[None, :],
        ssd_norm=p["ssd_norm"][l].astype(F32)[None, :],
        w_na=p["w_branch_na"][l].astype(BF16), w_swa=p["w_branch_swa"][l].astype(BF16),
        w_ssd=p["w_branch_ssd"][l].astype(BF16), w_out=p["w_out"][l].astype(BF16),
        norm_ffn=p["norm_ffn"][l][None, :], r_hi=r_hi, r_lo=r_lo,
        wg=p["expert_w_gate"][l].astype(BF16), wu=p["expert_w_up"][l].astype(BF16),
        wd=p["expert_w_down"][l].astype(BF16),
    )


def _layer(x, lp, cos, sin):
    bsz, seq, d = x.shape
    t = bsz * seq
    x2d = x.reshape(t, d)
    qkv_na = norm_proj_heads(x2d, lp["norm_mix"], lp["w_na_in"], lp["na_heads"], seq)
    qkv_swa = norm_proj_heads(x2d, lp["norm_mix"], lp["w_swa_in"], lp["swa_heads"], seq, cos, sin)
    zxbc, dt = norm_proj_ssd(x2d, lp["norm_mix"], lp["w_ssd_z"], lp["w_ssd_xbc"], lp["w_dt"], lp["conv_w"],
                             lp["conv_b"], seq)
    o_na = neighborhood_attention(qkv_na.reshape(bsz, seq, -1), lp["na_tab"])
    o_swa = sliding_window_attention(qkv_swa.reshape(bsz, seq, -1), lp["swa_sink"])
    o_ssd = ssd_mixer(zxbc.reshape(bsz, seq, -1), dt.reshape(bsz, seq, -1), lp["dtb_row"], lp["alog_row"],
                      lp["dskip"], lp["ssd_norm"])
    x_mid, aff = merge_and_route(x2d, lp["norm_mix"], lp["w_gate"], o_na.reshape(t, d), o_swa.reshape(t, d),
                                 o_ssd.reshape(t, d), lp["w_na"], lp["w_swa"], lp["w_ssd"], lp["w_out"],
                                 lp["norm_ffn"], lp["r_hi"], lp["r_lo"])
    return moe_layer(x_mid, aff, lp["norm_ffn"], lp["wg"], lp["wu"], lp["wd"]).reshape(bsz, seq, d)


def kernel(x_prompt, x_sample, norm_mix, w_in, na_q_norm, na_k_norm, na_rel_bias, swa_q_norm, swa_k_norm, swa_sink, ssd_conv_w, ssd_conv_b, ssd_dt_bias, ssd_a_log, ssd_d, ssd_norm, w_branch_na, w_branch_swa, w_branch_ssd, w_out, norm_ffn, router_w, expert_w_gate, expert_w_up, expert_w_down):
    p = dict(norm_mix=norm_mix, w_in=w_in, na_q_norm=na_q_norm, na_k_norm=na_k_norm, na_rel_bias=na_rel_bias,
             swa_q_norm=swa_q_norm, swa_k_norm=swa_k_norm, swa_sink=swa_sink, ssd_conv_w=ssd_conv_w,
             ssd_conv_b=ssd_conv_b, ssd_dt_bias=ssd_dt_bias, ssd_a_log=ssd_a_log, ssd_d=ssd_d, ssd_norm=ssd_norm,
             w_branch_na=w_branch_na, w_branch_swa=w_branch_swa, w_branch_ssd=w_branch_ssd, w_out=w_out,
             norm_ffn=norm_ffn, router_w=router_w, expert_w_gate=expert_w_gate, expert_w_up=expert_w_up,
             expert_w_down=expert_w_down)
    depth = w_in.shape[0]
    layers = [_prep_layer(p, l) for l in range(depth)]
    outs = []
    for x in (x_prompt, x_sample):
        cos, sin = rope_tables(x.shape[1])
        for lp in layers:
            x = _layer(x, lp, cos, sin)
        outs.append(x)
    return tuple(outs)
```

```python
import functools
import math

import jax
import jax.numpy as jnp
from jax import lax
from jax.experimental import pallas as pl
from jax.experimental.pallas import tpu as pltpu

F32 = jnp.float32
BF16 = jnp.bfloat16
I32 = jnp.int32

D_MODEL = 1024
HEAD_DIM = 64
GRID_W = 64
NA_HEADS = 16
NA_WIN_ROWS = 8
NA_WIN_COLS = 16
NA_QROWS = 4
NA_KROWS = 12
SWA_Q_HEADS = 16
SWA_KV_HEADS = 4
SWA_WINDOW = 128
SWA_BLOCK = 128
ROPE_THETA = 10000.0
SSD_HEADS = 16
SSD_GROUPS = 4
SSD_STATE = 128
SSD_CONV = 4
SSD_CHUNK = 128
SSD_INNER = 1024
N_EXPERTS = 16
EXPERT_FF = 2048
CAPACITY_FACTOR = 2
NORM_EPS = 1e-6
NEG_INF = -1e30
LANES = 128
VMEM_LIMIT = 56 * 1024 * 1024


def _cparams(sem):
    return pltpu.CompilerParams(dimension_semantics=sem, vmem_limit_bytes=VMEM_LIMIT)


def _rmsnorm(x, w):
    return x * lax.rsqrt(jnp.mean(x * x, axis=-1, keepdims=True) + NORM_EPS) * w


def _lo_mask(shape):
    return lax.broadcasted_iota(I32, shape, len(shape) - 1) < HEAD_DIM


def _headnorm2(x, w):
    lo = _lo_mask(x.shape)
    x2 = x * x
    sa = jnp.sum(jnp.where(lo, x2, 0.0), axis=-1, keepdims=True)
    sb = jnp.sum(jnp.where(lo, 0.0, x2), axis=-1, keepdims=True)
    ms = jnp.where(lo, sa, sb) * (1.0 / HEAD_DIM)
    return x * lax.rsqrt(ms + NORM_EPS) * w


def _stack_halves(x):
    lo = _lo_mask(x.shape)
    zero = jnp.zeros_like(x)
    return jnp.concatenate([jnp.where(lo, x, zero), jnp.where(lo, zero, x)], axis=0)


def _dot(a, b):
    return jnp.dot(a, b, preferred_element_type=F32)


def _dot_nt(a, b):
    return lax.dot_general(a, b, (((1,), (1,)), ((), ())), preferred_element_type=F32)


def _dot_tn(a, b):
    return lax.dot_general(a, b, (((0,), (0,)), ((), ())), preferred_element_type=F32)


def _split2(x):
    hi = x.astype(BF16)
    lo = (x - hi.astype(F32)).astype(BF16)
    return hi, lo


def _split3(x):
    a = x.astype(BF16)
    r = x - a.astype(F32)
    b = r.astype(BF16)
    c = (r - b.astype(F32)).astype(BF16)
    return a, b, c


def _norm_proj_kernel(*refs, n_out):
    x_ref, nw_ref = refs[0], refs[1]
    w_refs = refs[2:2 + n_out]
    o_refs = refs[2 + n_out:2 + 2 * n_out]
    h = _rmsnorm(x_ref[...], nw_ref[...]).astype(BF16)
    for w_ref, o_ref in zip(w_refs, o_refs):
        o_ref[...] = _dot(h, w_ref[...]).astype(o_ref.dtype)


def norm_proj(x2d, nw, weights, out_dtypes, tm=512):
    t, d = x2d.shape
    tm = min(tm, t)
    n_out = len(weights)
    return pl.pallas_call(
        functools.partial(_norm_proj_kernel, n_out=n_out),
        grid=(t // tm,),
        in_specs=[pl.BlockSpec((tm, d), lambda i: (i, 0)), pl.BlockSpec((1, d), lambda i: (0, 0))]
        + [pl.BlockSpec(w.shape, lambda i: (0, 0)) for w in weights],
        out_specs=[pl.BlockSpec((tm, w.shape[1]), lambda i: (i, 0)) for w in weights],
        out_shape=[jax.ShapeDtypeStruct((t, w.shape[1]), dt) for w, dt in zip(weights, out_dtypes)],
        compiler_params=_cparams(("parallel",)),
        name="norm_proj",
    )(x2d, nw, *weights)


def _rope(x, cos, sin_signed):
    first = (lax.broadcasted_iota(I32, x.shape, 1) % HEAD_DIM) < HEAD_DIM // 2
    partner = jnp.where(first, pltpu.roll(x, LANES - HEAD_DIM // 2, axis=1), pltpu.roll(x, HEAD_DIM // 2, axis=1))
    return x * cos + partner * sin_signed


PROJ_CHUNK = 512


def _norm_proj_heads_kernel(*refs, n_norm, rope):
    if rope:
        x_ref, nw_ref, w_ref, hw_ref, cos_ref, sin_ref, o_ref = refs
    else:
        x_ref, nw_ref, w_ref, hw_ref, o_ref = refs
    h = _rmsnorm(x_ref[...], nw_ref[...]).astype(BF16)
    ncols = w_ref.shape[1]
    for c0 in range(0, ncols, PROJ_CHUNK):
        acc = _dot(h, w_ref[:, c0:c0 + PROJ_CHUNK])
        parts = []
        for j in range(PROJ_CHUNK // LANES):
            col = c0 // LANES + j
            blk = acc[:, j * LANES:(j + 1) * LANES]
            if col < n_norm:
                blk = _headnorm2(blk, hw_ref[:, col * LANES:(col + 1) * LANES])
                if rope:
                    blk = _rope(blk, cos_ref[...], sin_ref[...])
            parts.append(blk)
        o_ref[:, c0:c0 + PROJ_CHUNK] = jnp.concatenate(parts, axis=1).astype(o_ref.dtype)


def norm_proj_heads(x2d, nw, w, head_w, seq, cos=None, sin=None, tm=512):
    t, d = x2d.shape
    tm = min(tm, t, seq)
    n = w.shape[1]
    n_norm = head_w.shape[1] // LANES
    rope = cos is not None
    in_specs = [pl.BlockSpec((tm, d), lambda i: (i, 0)), pl.BlockSpec((1, d), lambda i: (0, 0)),
                pl.BlockSpec(w.shape, lambda i: (0, 0)), pl.BlockSpec(head_w.shape, lambda i: (0, 0))]
    args = [x2d, nw, w, head_w]
    if rope:
        per_seq = seq // tm
        in_specs += [pl.BlockSpec((tm, LANES), lambda i: (i % per_seq, 0))] * 2
        args += [cos, sin]
    return pl.pallas_call(
        functools.partial(_norm_proj_heads_kernel, n_norm=n_norm, rope=rope),
        grid=(t // tm,),
        in_specs=in_specs,
        out_specs=pl.BlockSpec((tm, n), lambda i: (i, 0)),
        out_shape=jax.ShapeDtypeStruct((t, n), BF16),
        compiler_params=_cparams(("parallel",)),
        name="norm_proj_heads",
    )(*args)


NA_PAIRS = 4


def _na_kernel(q_ref, k0_ref, k1_ref, k2_ref, v0_ref, v1_ref, v2_ref, tab_ref, o_ref):
    g = pl.program_id(2)
    ng = pl.num_programs(2)
    variant = jnp.where(g == 0, 0, jnp.where(g == ng - 1, 2, 1))
    nk = NA_KROWS * GRID_W
    outs = []
    for hp in range(NA_PAIRS):
        cols = slice(hp * LANES, (hp + 1) * LANES)
        kw = jnp.concatenate([k0_ref[0, :, cols], k1_ref[0, :, cols], k2_ref[0, :, cols]], axis=0)
        s = _dot_nt(q_ref[0, :, cols], _stack_halves(kw)) + tab_ref[hp, variant]
        sa, sb = s[:, :nk], s[:, nk:]
        pa = jnp.exp(sa - jnp.max(sa, axis=-1, keepdims=True))
        pb = jnp.exp(sb - jnp.max(sb, axis=-1, keepdims=True))
        la = jnp.sum(pa, axis=-1, keepdims=True)
        lb = jnp.sum(pb, axis=-1, keepdims=True)
        p = jnp.concatenate([pa, pb], axis=1).astype(BF16)
        vw = jnp.concatenate([v0_ref[0, :, cols], v1_ref[0, :, cols], v2_ref[0, :, cols]], axis=0)
        o = _dot(p, _stack_halves(vw))
        outs.append(o * jnp.where(_lo_mask(o.shape), 1.0 / la, 1.0 / lb))
    o_ref[0] = jnp.concatenate(outs, axis=1).astype(o_ref.dtype)


def na_bias_table(rel_bias):
    nh = rel_bias.shape[0]
    offs = jnp.array([0, NA_QROWS, 2 * NA_QROWS], I32)
    r0w = jnp.array([[0, 0, 0, 0], [0, 1, 2, 3], [4, 4, 4, 4]], I32)
    j = jnp.arange(NA_QROWS, dtype=I32)
    i = jnp.arange(NA_KROWS, dtype=I32)
    c = jnp.arange(GRID_W, dtype=I32)
    qrow = offs[:, None] + j[None, :]
    row_rel = i[None, None, :] - qrow[:, :, None] + NA_WIN_ROWS - 1
    row_ok = (i[None, None, :] >= r0w[:, :, None]) & (i[None, None, :] < r0w[:, :, None] + NA_WIN_ROWS)
    c0 = jnp.clip(c - NA_WIN_COLS // 2, 0, GRID_W - NA_WIN_COLS)
    col_rel = c[None, :] - c[:, None] + NA_WIN_COLS - 1
    col_ok = (c[None, :] >= c0[:, None]) & (c[None, :] < c0[:, None] + NA_WIN_COLS)
    row_sel = (row_rel[..., None] == jnp.arange(2 * NA_WIN_ROWS - 1)).astype(F32)
    col_sel = (col_rel[..., None] == jnp.arange(2 * NA_WIN_COLS - 1)).astype(F32)
    by_col = jnp.einsum('hab,ckb->hack', rel_bias.astype(F32), col_sel, precision=lax.Precision.HIGHEST)
    b = jnp.einsum('vjia,hack->hvjcik', row_sel, by_col, precision=lax.Precision.HIGHEST)
    ok = row_ok[:, :, None, :, None] & col_ok[None, None, :, None, :]
    b = jnp.where(ok[None], b, NEG_INF)
    b = b.reshape(nh // 2, 2, 3, NA_QROWS * GRID_W, NA_KROWS * GRID_W)
    return jnp.concatenate([b[:, 0], b[:, 1]], axis=-1)


def neighborhood_attention(qkv, tab):
    bsz, seq, _ = qkv.shape
    rows = seq // GRID_W
    assert rows % NA_QROWS == 0 and rows >= NA_KROWS + NA_QROWS
    ng = rows // NA_QROWS
    tq = NA_QROWS * GRID_W
    wblk = NA_PAIRS * LANES
    nblk = NA_HEADS * HEAD_DIM // wblk

    def kv_spec(col0, j):
        return pl.BlockSpec((1, tq, wblk), lambda hp, b, g: (b, jnp.clip(g - 1, 0, ng - 3) + j, col0 + hp))

    return pl.pallas_call(
        _na_kernel,
        grid=(nblk, bsz, ng),
        in_specs=[pl.BlockSpec((1, tq, wblk), lambda hp, b, g: (b, g, hp))]
        + [kv_spec(nblk, j) for j in range(3)] + [kv_spec(2 * nblk, j) for j in range(3)]
        + [pl.BlockSpec((NA_PAIRS, 3, tq, 2 * NA_KROWS * GRID_W), lambda hp, b, g: (hp, 0, 0, 0),
                        pipeline_mode=pl.Buffered(1))],
        out_specs=pl.BlockSpec((1, tq, wblk), lambda hp, b, g: (b, g, hp)),
        out_shape=jax.ShapeDtypeStruct((bsz, seq, NA_HEADS * HEAD_DIM), BF16),
        compiler_params=_cparams(("parallel", "parallel", "parallel")),
        name="neighborhood_attention",
    )(qkv, qkv, qkv, qkv, qkv, qkv, qkv, tab)


def _swa_kernel(sink_ref, q_ref, kp_ref, kc_ref, kx_ref, vp_ref, vc_ref, vx_ref, o_ref, *, seq):
    i = pl.program_id(1)
    blk = SWA_BLOCK
    span = 3 * blk
    qpos = i * blk + lax.broadcasted_iota(I32, (2 * blk, span), 0) % blk
    kpos = (i - 1) * blk + lax.broadcasted_iota(I32, (2 * blk, span), 1)
    valid = (jnp.abs(kpos - qpos) <= SWA_WINDOW) & (kpos >= 0) & (kpos < seq)
    upper = lax.broadcasted_iota(I32, (2 * blk, 1), 0) < blk
    group = SWA_Q_HEADS // SWA_KV_HEADS
    qw = group * HEAD_DIM

    def softmax_part(sc, sink):
        sc = jnp.where(valid, sc, NEG_INF)
        m = jnp.maximum(jnp.max(sc, axis=-1, keepdims=True), sink)
        e = jnp.exp(sc - m)
        return e, jnp.sum(e, axis=-1, keepdims=True) + jnp.exp(sink - m)

    outs = []
    for kvh in range(SWA_KV_HEADS):
        q = q_ref[0, :, kvh * qw:(kvh + 1) * qw]
        q2 = jnp.concatenate([q[:, :LANES], q[:, LANES:]], axis=0)
        cols = slice(kvh * LANES, (kvh + 1) * LANES)
        kk = jnp.concatenate([kp_ref[0, :, cols], kc_ref[0, :, cols], kx_ref[0, :, cols]], axis=0)
        s = _dot_nt(q2, _stack_halves(kk))
        h0 = group * kvh
        ea, la = softmax_part(s[:, :span], jnp.where(upper, sink_ref[h0], sink_ref[h0 + 2]))
        eb, lb = softmax_part(s[:, span:], jnp.where(upper, sink_ref[h0 + 1], sink_ref[h0 + 3]))
        p = jnp.concatenate([ea, eb], axis=1).astype(BF16)
        vv = jnp.concatenate([vp_ref[0, :, cols], vc_ref[0, :, cols], vx_ref[0, :, cols]], axis=0)
        o = _dot(p, _stack_halves(vv))
        o = o * jnp.where(_lo_mask(o.shape), 1.0 / la, 1.0 / lb)
        outs.append(jnp.concatenate([o[:blk], o[blk:]], axis=1))
    o_ref[0] = jnp.concatenate(outs, axis=1).astype(o_ref.dtype)


def rope_tables(seq):
    half = HEAD_DIM // 2
    inv = ROPE_THETA ** (-jnp.arange(half, dtype=F32) / half)
    ang = jnp.arange(seq, dtype=F32)[:, None] * inv[None, :]
    cos, sin = jnp.cos(ang), jnp.sin(ang)
    return jnp.tile(cos, (1, 4)), jnp.tile(jnp.concatenate([-sin, sin], axis=1), (1, 2))


def sliding_window_attention(qkv, sink):
    bsz, seq, _ = qkv.shape
    blk = SWA_BLOCK
    nblk = seq // blk
    qcols = SWA_Q_HEADS * HEAD_DIM
    kvw = SWA_KV_HEADS * LANES

    def kv_spec(col, j):
        return pl.BlockSpec((1, blk, kvw), lambda b, i: (b, jnp.clip(i + j - 1, 0, nblk - 1), col))

    return pl.pallas_call(
        functools.partial(_swa_kernel, seq=seq),
        grid=(bsz, nblk),
        in_specs=[pl.BlockSpec(memory_space=pltpu.SMEM), pl.BlockSpec((1, blk, qcols), lambda b, i: (b, i, 0))]
        + [kv_spec(qcols // kvw, j) for j in range(3)] + [kv_spec(qcols // kvw + 1, j) for j in range(3)],
        out_specs=pl.BlockSpec((1, blk, qcols), lambda b, i: (b, i, 0)),
        out_shape=jax.ShapeDtypeStruct((bsz, seq, qcols), BF16),
        compiler_params=_cparams(("parallel", "parallel")),
        name="sliding_window_attention",
    )(sink, qkv, qkv, qkv, qkv, qkv, qkv, qkv)


SSD_STEP_CHUNKS = 2
SSD_HALO = 16


def _conv_silu(main_ref, prev_ref, next_ref, w_ref, b_ref, has_prev, has_next):
    xm = main_ref[0].astype(F32)
    n = xm.shape[0]
    pv = prev_ref[0].astype(F32) * has_prev
    nx = next_ref[0].astype(F32) * has_next
    hp = pv.shape[0]
    r8 = lax.broadcasted_iota(I32, (8, 1), 0)

    def shifted(k, edge):
        rolled = pltpu.roll(xm, k % n, axis=0)
        if k > 0:
            return jnp.concatenate([edge(rolled[:8]), rolled[8:]], axis=0)
        return jnp.concatenate([rolled[:n - 8], edge(rolled[n - 8:])], axis=0)

    xm1 = shifted(1, lambda t: jnp.where(r8 == 0, pv[hp - 1:hp], t))
    xm2 = shifted(2, lambda t: jnp.where(r8 == 0, pv[hp - 2:hp - 1], jnp.where(r8 == 1, pv[hp - 1:hp], t)))
    xp1 = shifted(-1, lambda t: jnp.where(r8 == 7, nx[0:1], t))
    w = w_ref[...]
    y = w[0:1] * xm2 + w[1:2] * xm1 + w[2:3] * xm + w[3:4] * xp1 + b_ref[...]
    return y * jax.nn.sigmoid(y)


def _ssd_chunk(xs, bc, dt_raw, h, dtb, alog, e_mat, reverse):
    ln = SSD_CHUNK
    doff = SSD_HEADS if reverse else 0
    gw = SSD_INNER // SSD_GROUPS
    dt = jax.nn.softplus(dt_raw + dtb)
    a_dt = dt * (-jnp.exp(alog))
    tr = lax.broadcasted_iota(I32, (ln, ln), 0)
    tc = lax.broadcasted_iota(I32, (ln, ln), 1)
    scanned = (tc >= tr) if reverse else (tc <= tr)
    tri = scanned.astype(BF16)
    a1, a2, a3 = _split3(a_dt)
    cum = _dot(tri, a1) + _dot(tri, a2) + _dot(tri, a3)
    tot = jnp.sum(a_dt, axis=0, keepdims=True)
    cum_t = cum.T

    stacked = jnp.concatenate([dt, jnp.exp(tot - cum), jnp.exp(cum), jnp.broadcast_to(jnp.exp(tot), (8, LANES))],
                              axis=0)
    s_hi, s_lo = _split2(stacked)
    expanded = _dot(s_hi, e_mat) + _dot(s_lo, e_mat)
    dt_e, wst_e, ecum_e, etot_e = (expanded[:ln], expanded[ln:2 * ln], expanded[2 * ln:3 * ln],
                                   expanded[3 * ln:3 * ln + 1])
    xdt = xs * dt_e
    xw = (xdt * wst_e).astype(BF16)
    xdt_b = xdt.astype(BF16)

    bm = bc[:, :SSD_GROUPS * SSD_STATE].astype(BF16)
    cm = bc[:, SSD_GROUPS * SSD_STATE:].astype(BF16)
    y_parts = []
    new_states = []
    for g in range(SSD_GROUPS):
        b_g = bm[:, g * SSD_STATE:(g + 1) * SSD_STATE]
        c_g = cm[:, g * SSD_STATE:(g + 1) * SSD_STATE]
        cb = _dot_nt(c_g, b_g)
        h_g = h[:, g * gw:(g + 1) * gw]
        y_off = _dot(c_g, h_g.astype(BF16)) * ecum_e[:, g * gw:(g + 1) * gw]
        for pair in range(2):
            ms = []
            for hh in range(2):
                lane = doff + 4 * g + 2 * pair + hh
                seg = cum[:, lane:lane + 1] - cum_t[lane:lane + 1, :]
                ms.append((cb * jnp.exp(jnp.where(scanned, seg, NEG_INF))).astype(BF16))
            k = 2 * g + pair
            y_parts.append(_dot(jnp.concatenate(ms, axis=1), _stack_halves(xdt_b[:, k * LANES:(k + 1) * LANES]))
                           + y_off[:, pair * LANES:(pair + 1) * LANES])
        new_states.append(h_g * etot_e[:, g * gw:(g + 1) * gw] + _dot_tn(b_g, xw[:, g * gw:(g + 1) * gw]))
    return jnp.concatenate(y_parts, axis=1), jnp.concatenate(new_states, axis=1)


def _ssd_kernel(*refs, reverse):
    if reverse:
        (xs_ref, bc_ref, dt_ref, dtb_ref, alog_ref, e_ref, z_ref, yprev_ref, dskip_ref, nw_ref,
         o_ref, h_ref) = refs
    else:
        (x_ref, xp_ref, xn_ref, bc_ref, bcp_ref, bcn_ref, dt_ref, cwx_ref, cbx_ref, cwb_ref, cbb_ref,
         dtb_ref, alog_ref, e_ref, o_ref, xs_out_ref, bc_out_ref, h_ref) = refs
    step = pl.program_id(1)
    nb = pl.num_programs(1)
    ln = SSD_CHUNK
    gw = SSD_INNER // SSD_GROUPS

    @pl.when(step == 0)
    def _():
        h_ref[...] = jnp.zeros_like(h_ref)

    if reverse:
        xs_all = xs_ref[0].astype(F32)
        bc_all = bc_ref[0].astype(F32)
    else:
        has_prev = (step > 0).astype(F32)
        has_next = (step < nb - 1).astype(F32)
        xs_all = _conv_silu(x_ref, xp_ref, xn_ref, cwx_ref, cbx_ref, has_prev, has_next)
        bc_all = _conv_silu(bc_ref, bcp_ref, bcn_ref, cwb_ref, cbb_ref, has_prev, has_next)
        xs_out_ref[0] = xs_all.astype(xs_out_ref.dtype)
        bc_out_ref[0] = bc_all.astype(bc_out_ref.dtype)

    h = h_ref[...]
    ys = [None] * SSD_STEP_CHUNKS
    order = range(SSD_STEP_CHUNKS - 1, -1, -1) if reverse else range(SSD_STEP_CHUNKS)
    for ci in order:
        rows = slice(ci * ln, (ci + 1) * ln)
        ys[ci], h = _ssd_chunk(xs_all[rows], bc_all[rows], dt_ref[0, rows, :], h, dtb_ref[...], alog_ref[...],
                               e_ref[...], reverse)
    h_ref[...] = h
    y = jnp.concatenate(ys, axis=0)

    if reverse:
        y = y + yprev_ref[0].astype(F32) + dskip_ref[...] * xs_all
        z = z_ref[0].astype(F32)
        y = y * (z * jax.nn.sigmoid(z))
        outs = []
        for g in range(SSD_GROUPS):
            yg = y[:, g * gw:(g + 1) * gw]
            outs.append(yg * lax.rsqrt(jnp.mean(yg * yg, axis=-1, keepdims=True) + NORM_EPS))
        y = jnp.concatenate(outs, axis=1) * nw_ref[...]
    o_ref[0] = y.astype(o_ref.dtype)


def _ssd_forward(zxbc, dt, conv_w, conv_b, dtb_row, alog_row):
    bsz, seq, _ = zxbc.shape
    rows = SSD_STEP_CHUNKS * SSD_CHUNK
    nb = seq // rows
    hb = rows // SSD_HALO

    def main_spec(col):
        return pl.BlockSpec((1, rows, SSD_INNER), lambda b, s: (b, s, col))

    def prev_spec(col):
        return pl.BlockSpec((1, SSD_HALO, SSD_INNER), lambda b, s: (b, jnp.maximum(s * hb - 1, 0), col))

    def next_spec(col):
        return pl.BlockSpec((1, SSD_HALO, SSD_INNER),
                            lambda b, s: (b, jnp.minimum(s * hb + hb, seq // SSD_HALO - 1), col))

    def const_spec(shape, col=0):
        return pl.BlockSpec(shape, lambda b, s: (0, col))

    act = jax.ShapeDtypeStruct((bsz, seq, SSD_INNER), BF16)
    return pl.pallas_call(
        functools.partial(_ssd_kernel, reverse=False),
        grid=(bsz, nb),
        in_specs=[main_spec(1), prev_spec(1), next_spec(1), main_spec(2), prev_spec(2), next_spec(2),
                  pl.BlockSpec((1, rows, LANES), lambda b, s: (b, s, 0)),
                  const_spec((SSD_CONV, SSD_INNER), 0), const_spec((1, SSD_INNER), 0),
                  const_spec((SSD_CONV, SSD_INNER), 1), const_spec((1, SSD_INNER), 1),
                  const_spec((1, LANES)), const_spec((1, LANES)), const_spec((LANES, SSD_INNER))],
        out_specs=[main_spec(0)] * 3,
        out_shape=[act, act, act],
        scratch_shapes=[pltpu.VMEM((SSD_STATE, SSD_INNER), F32)],
        compiler_params=_cparams(("parallel", "arbitrary")),
        name="ssd_fwd",
    )(zxbc, zxbc, zxbc, zxbc, zxbc, zxbc, dt, conv_w, conv_b, conv_w, conv_b, dtb_row, alog_row,
      ssd_expand_matrix(False))


def _ssd_backward(zxbc, xs, bc, dt, dtb_row, alog_row, y_fwd, dskip, nw):
    bsz, seq, _ = zxbc.shape
    rows = SSD_STEP_CHUNKS * SSD_CHUNK
    nb = seq // rows
    main = pl.BlockSpec((1, rows, SSD_INNER), lambda b, s: (b, nb - 1 - s, 0))

    def const_spec(shape):
        return pl.BlockSpec(shape, lambda b, s: (0, 0))

    return pl.pallas_call(
        functools.partial(_ssd_kernel, reverse=True),
        grid=(bsz, nb),
        in_specs=[main, main, pl.BlockSpec((1, rows, LANES), lambda b, s: (b, nb - 1 - s, 0)),
                  const_spec((1, LANES)), const_spec((1, LANES)), const_spec((LANES, SSD_INNER)),
                  main, main, const_spec((1, SSD_INNER)), const_spec((1, SSD_INNER))],
        out_specs=main,
        out_shape=jax.ShapeDtypeStruct((bsz, seq, SSD_INNER), BF16),
        scratch_shapes=[pltpu.VMEM((SSD_STATE, SSD_INNER), F32)],
        compiler_params=_cparams(("parallel", "arbitrary")),
        name="ssd_bwd_final",
    )(xs, bc, dt, dtb_row, alog_row, ssd_expand_matrix(True), zxbc, y_fwd, dskip, nw)


def ssd_expand_matrix(reverse):
    rows = jnp.arange(LANES)[:, None]
    head = jnp.arange(SSD_INNER)[None, :] // HEAD_DIM
    return (rows == head + (SSD_HEADS if reverse else 0)).astype(BF16)


def ssd_mixer(zxbc, dt, conv_w, conv_b, dtb_row, alog_row, dskip, nw):
    y_f, xs, bc = _ssd_forward(zxbc, dt, conv_w, conv_b, dtb_row, alog_row)
    return _ssd_backward(zxbc, xs, bc, dt, dtb_row, alog_row, y_f, dskip, nw)


def _merge_kernel(x_ref, nw_ref, wg_ref, ona_ref, oswa_ref, ossd_ref, wna_ref, wswa_ref, wssd_ref, wout_ref,
                  nffn_ref, rhi_ref, rlo_ref, xo_ref, aff_ref):
    x = x_ref[...]
    h = _rmsnorm(x, nw_ref[...]).astype(BF16)
    d = x.shape[1]
    gates = jax.nn.sigmoid(_dot(h, wg_ref[...]))
    merged = (gates[:, :d] * _dot(ona_ref[...], wna_ref[...])
              + gates[:, d:2 * d] * _dot(oswa_ref[...], wswa_ref[...])
              + gates[:, 2 * d:] * _dot(ossd_ref[...], wssd_ref[...]))
    xn = x + _dot(merged.astype(BF16), wout_ref[...])
    xo_ref[...] = xn
    h2 = _rmsnorm(xn, nffn_ref[...])
    h_hi, h_lo = _split2(h2)
    logits = _dot(h_hi, rhi_ref[...]) + _dot(h_hi, rlo_ref[...]) + _dot(h_lo, rhi_ref[...])
    live = lax.broadcasted_iota(I32, logits.shape, 1) < N_EXPERTS
    logits = jnp.where(live, logits, NEG_INF)
    e = jnp.exp(logits - jnp.max(logits, axis=-1, keepdims=True))
    aff_ref[...] = e / jnp.sum(e, axis=-1, keepdims=True)


def merge_and_route(x2d, nw, wg, o_na, o_swa, o_ssd, w_na, w_swa, w_ssd, w_out, nffn, r_hi, r_lo, tm=256):
    t, d = x2d.shape
    tm = min(tm, t)
    row = lambda i: (i, 0)
    fixed = lambda i: (0, 0)
    return pl.pallas_call(
        _merge_kernel,
        grid=(t // tm,),
        in_specs=[pl.BlockSpec((tm, d), row), pl.BlockSpec((1, d), fixed), pl.BlockSpec(wg.shape, fixed),
                  pl.BlockSpec((tm, d), row), pl.BlockSpec((tm, d), row), pl.BlockSpec((tm, d), row),
                  pl.BlockSpec((d, d), fixed), pl.BlockSpec((d, d), fixed), pl.BlockSpec((d, d), fixed),
                  pl.BlockSpec((d, d), fixed), pl.BlockSpec((1, d), fixed),
                  pl.BlockSpec((d, LANES), fixed), pl.BlockSpec((d, LANES), fixed)],
        out_specs=[pl.BlockSpec((tm, d), row), pl.BlockSpec((tm, LANES), row)],
        out_shape=[jax.ShapeDtypeStruct((t, d), F32), jax.ShapeDtypeStruct((t, LANES), F32)],
        compiler_params=_cparams(("parallel",)),
        name="merge_and_route",
    )(x2d, nw, wg, o_na, o_swa, o_ssd, w_na, w_swa, w_ssd, w_out, nffn, r_hi, r_lo)


def _prefix_rows(mask_f, upper, lower_strict):
    lp = _dot(mask_f.astype(BF16), upper)
    rc = jnp.broadcast_to(lp[:, LANES - 1:LANES], lp.shape)
    return lp, rc, _dot(lower_strict, rc.astype(BF16))


def _select_kernel(a_ref, idx_ref, wts_ref, slot_ref, rstart_ref, *, cap, chunk):
    a = a_ref[0]
    nr = a.shape[0]
    bits = pltpu.bitcast(a, I32)

    def search(i, v):
        cand = v | lax.shift_left(jnp.int32(1), 30 - i)
        cnt = jnp.sum((bits >= cand).astype(F32))
        return jnp.where(cnt >= cap, cand, v)

    thr = lax.fori_loop(0, 31, search, jnp.int32(0))
    gt = bits > thr
    eq = bits == thr
    need = cap - jnp.sum(gt.astype(F32))

    li = lax.broadcasted_iota(I32, (LANES, LANES), 0)
    lj = lax.broadcasted_iota(I32, (LANES, LANES), 1)
    upper = (li <= lj).astype(BF16)
    ri = lax.broadcasted_iota(I32, (nr, nr), 0)
    rj = lax.broadcasted_iota(I32, (nr, nr), 1)
    lower_strict = (rj < ri).astype(BF16)

    elp, _, ers = _prefix_rows(eq.astype(F32), upper, lower_strict)
    sel = gt | (eq & (elp + ers <= need))
    lp, rc, rs = _prefix_rows(sel.astype(F32), upper, lower_strict)
    slot_ref[0] = jnp.where(sel, lp + rs - 1.0, -1.0).astype(I32)
    rstart_ref[0] = rs.astype(I32)

    row_end = rs[:, 0:1] + rc[:, 0:1]
    rcount = rc[:, 0:1]
    lp_b = lp.astype(BF16)
    a1, a2, a3 = _split3(a)
    sub_r = lax.broadcasted_iota(I32, (nr, chunk), 0).astype(F32)
    sub_l = lax.broadcasted_iota(I32, (LANES, chunk), 0).astype(F32)
    for ch in range(cap // chunk):
        s = (ch * chunk + lax.broadcasted_iota(I32, (1, chunk), 1)).astype(F32)
        before = (row_end <= s).astype(F32)
        row_of = jnp.sum(before, axis=0, keepdims=True)
        j = s - jnp.sum(before * rcount, axis=0, keepdims=True)
        onehot = (sub_r == row_of).astype(BF16)
        g = _dot_tn(lp_b, onehot)
        lane_of = jnp.sum((g <= j).astype(F32), axis=0, keepdims=True)
        idx_ref[0, :, ch * chunk:(ch + 1) * chunk] = (row_of * LANES + lane_of).astype(I32)
        ga = _dot_tn(a1, onehot) + _dot_tn(a2, onehot) + _dot_tn(a3, onehot)
        wts_ref[0, :, ch * chunk:(ch + 1) * chunk] = jnp.sum(jnp.where(sub_l == lane_of, ga, 0.0), axis=0,
                                                               keepdims=True)


def select_tokens(aff_t, cap):
    ne, nr, _ = aff_t.shape
    chunk = min(1024, cap)
    blk = lambda e: (e, 0, 0)
    return pl.pallas_call(
        functools.partial(_select_kernel, cap=cap, chunk=chunk),
        grid=(ne,),
        in_specs=[pl.BlockSpec((1, nr, LANES), blk)],
        out_specs=[pl.BlockSpec((1, 1, cap), blk), pl.BlockSpec((1, 1, cap), blk),
                   pl.BlockSpec((1, nr, LANES), blk), pl.BlockSpec((1, nr, LANES), blk)],
        out_shape=[jax.ShapeDtypeStruct((ne, 1, cap), I32), jax.ShapeDtypeStruct((ne, 1, cap), F32),
                   jax.ShapeDtypeStruct((ne, nr, LANES), I32), jax.ShapeDtypeStruct((ne, nr, LANES), I32)],
        compiler_params=_cparams(("parallel",)),
        name="select_tokens",
    )(aff_t)


def _expert_kernel(idx_hbm, x_hbm, w_ref, nffn_ref, wg_ref, wu_ref, wd_ref, o_ref, idx_smem, xbuf0, xbuf1, isem,
                   gsem, *, tm):
    nt = pl.num_programs(1)
    n = pl.program_id(0) * nt + pl.program_id(1)
    total = pl.num_programs(0) * nt
    xbufs = (xbuf0, xbuf1)

    def idx_copy(step, s):
        return pltpu.make_async_copy(idx_hbm.at[step], idx_smem.at[s], isem.at[s])

    def gather_wait(s):
        pltpu.make_async_copy(x_hbm.at[pl.ds(0, tm), :], xbufs[s], gsem.at[s]).wait()

    def row_copy(s, r):
        tok = idx_smem[s, r]
        return pltpu.make_async_copy(x_hbm.at[pl.ds(tok, 1), :], xbufs[s].at[pl.ds(r, 1), :], gsem.at[s])

    @pl.when(n == 0)
    def _():
        idx_copy(0, 0).start()
        idx_copy(0, 0).wait()

        def body(r, carry):
            row_copy(0, r).start()
            return carry
        lax.fori_loop(0, tm, body, 0)
        idx_copy(jnp.minimum(1, total - 1), 1).start()

    def step(s):
        o = 1 - s
        gather_wait(s)
        idx_copy(jnp.minimum(n + 1, total - 1), o).wait()
        for r in range(tm):
            row_copy(o, r).start(priority=r % 2)
        idx_copy(jnp.minimum(n + 2, total - 1), s).start()
        h = _rmsnorm(xbufs[s][...], nffn_ref[...]).astype(BF16)
        gate = _dot(h, wg_ref[0])
        up = _dot(h, wu_ref[0])
        act = (gate * jax.nn.sigmoid(gate) * up).astype(BF16)
        y = _dot(act, wd_ref[0])
        eye = lax.broadcasted_iota(I32, (LANES, LANES), 0) == lax.broadcasted_iota(I32, (LANES, LANES), 1)
        wcol = jnp.concatenate(
            [jnp.sum(jnp.where(eye, jnp.broadcast_to(w_ref[0, r:r + 1, :], (LANES, LANES)), 0.0), axis=1,
                     keepdims=True) for r in range(tm // LANES)], axis=0)
        o_ref[...] = (y * wcol).astype(o_ref.dtype)

        @pl.when(n == total - 1)
        def _():
            gather_wait(o)
            idx_copy(0, s).wait()

    for s in range(2):
        pl.when(n % 2 == s)(functools.partial(step, s))


def run_experts(idx_tiles, x2d, wts_rows, nffn, wg, wu, wd, cap, tm):
    ne = wg.shape[0]
    t, d = x2d.shape
    nt = cap // tm
    ff = wg.shape[2]
    return pl.pallas_call(
        functools.partial(_expert_kernel, tm=tm),
        grid=(ne, nt),
        in_specs=[pl.BlockSpec(memory_space=pl.ANY), pl.BlockSpec(memory_space=pl.ANY),
                  pl.BlockSpec((1, tm // LANES, LANES), lambda e, i: (e * nt + i, 0, 0)),
                  pl.BlockSpec((1, d), lambda e, i: (0, 0)),
                  pl.BlockSpec((1, d, ff), lambda e, i: (e, 0, 0)),
                  pl.BlockSpec((1, d, ff), lambda e, i: (e, 0, 0)),
                  pl.BlockSpec((1, ff, d), lambda e, i: (e, 0, 0))],
        out_specs=pl.BlockSpec((tm, d), lambda e, i: (e * nt + i, 0)),
        out_shape=jax.ShapeDtypeStruct((ne * cap, d), BF16),
        scratch_shapes=[pltpu.SMEM((2, tm), I32), pltpu.VMEM((tm, d), F32), pltpu.VMEM((tm, d), F32),
                        pltpu.SemaphoreType.DMA((2,)), pltpu.SemaphoreType.DMA((2,))],
        compiler_params=_cparams(("arbitrary", "arbitrary")),
        name="run_experts",
    )(idx_tiles, x2d, wts_rows, nffn, wg, wu, wd)


COMB_ALIGN = 16
COMB_SMALL = 64
COMB_BIG = LANES + COMB_ALIGN


def _combine_kernel(rstart_ref, x_ref, slot_ref, ye_hbm, o_ref, sbuf, bbuf, ssem, bsem, *, cap):
    r = pl.program_id(0)
    nr = pl.num_programs(0)
    ne = N_EXPERTS
    cur = r % 2

    def start_of(row, e, win):
        p = e * cap + rstart_ref[row * ne + e]
        return jnp.minimum((p // COMB_ALIGN) * COMB_ALIGN, ne * cap - win)

    def fits_small(row):
        ok = None
        for e in range(ne):
            end = e * cap + rstart_ref[(row + 1) * ne + e]
            fit = end - start_of(row, e, COMB_SMALL) <= COMB_SMALL
            ok = fit if ok is None else (ok & fit)
        return ok

    def small_copy(row, e, s):
        st = pl.multiple_of(start_of(row, e, COMB_SMALL), COMB_ALIGN)
        return pltpu.make_async_copy(ye_hbm.at[pl.ds(st, COMB_SMALL), :],
                                     sbuf.at[s, pl.ds(e * COMB_SMALL, COMB_SMALL), :], ssem.at[s, e])

    def big_copy(row, e, s):
        st = pl.multiple_of(start_of(row, e, COMB_BIG), COMB_ALIGN)
        return pltpu.make_async_copy(ye_hbm.at[pl.ds(st, COMB_BIG), :], bbuf.at[s, e], bsem.at[s, e])

    def issue(row, s):
        small = fits_small(row)

        @pl.when(small)
        def _():
            for e in range(ne):
                small_copy(row, e, s).start()

        @pl.when(jnp.logical_not(small))
        def _():
            for e in range(ne):
                big_copy(row, e, s).start()

    @pl.when(r == 0)
    def _():
        issue(0, 0)

    @pl.when(r + 1 < nr)
    def _():
        issue(r + 1, 1 - cur)

    small = fits_small(r)
    slots = slot_ref[...]

    def rel(e, win, shift):
        sl = slots[:, e:e + 1]
        return jnp.where(sl >= 0, sl + (e * cap + shift - start_of(r, e, win)), -1)

    @pl.when(small)
    def _():
        for e in range(ne):
            small_copy(r, e, cur).wait()
        lane = lax.broadcasted_iota(I32, (LANES, LANES), 1)
        pieces = [(lane == jnp.where(lane < COMB_SMALL, rel(2 * k, COMB_SMALL, 0),
                                     rel(2 * k + 1, COMB_SMALL, COMB_SMALL))).astype(BF16)
                  for k in range(ne // 2)]
        o_ref[...] = x_ref[...] + _dot(jnp.concatenate(pieces, axis=1), sbuf[cur])

    @pl.when(jnp.logical_not(small))
    def _():
        acc = x_ref[...]
        lane = lax.broadcasted_iota(I32, (LANES, COMB_BIG), 1)
        for e in range(ne):
            big_copy(r, e, cur).wait()
            acc = acc + _dot((lane == rel(e, COMB_BIG, 0)).astype(BF16), bbuf[cur, e])
        o_ref[...] = acc


def combine(x2d, slot_t, rstart_flat, ye, cap):
    t, d = x2d.shape
    nr = t // LANES
    ne = N_EXPERTS
    grid_spec = pltpu.PrefetchScalarGridSpec(
        num_scalar_prefetch=1,
        grid=(nr,),
        in_specs=[pl.BlockSpec((LANES, d), lambda r, rs: (r, 0)),
                  pl.BlockSpec((LANES, ne), lambda r, rs: (r, 0)),
                  pl.BlockSpec(memory_space=pl.ANY)],
        out_specs=pl.BlockSpec((LANES, d), lambda r, rs: (r, 0)),
        scratch_shapes=[pltpu.VMEM((2, ne * COMB_SMALL, d), BF16), pltpu.VMEM((2, ne, COMB_BIG, d), BF16),
                        pltpu.SemaphoreType.DMA((2, ne)), pltpu.SemaphoreType.DMA((2, ne))],
    )
    return pl.pallas_call(
        functools.partial(_combine_kernel, cap=cap),
        grid_spec=grid_spec,
        out_shape=jax.ShapeDtypeStruct((t, d), F32),
        compiler_params=_cparams(("arbitrary",)),
        name="combine",
    )(rstart_flat, x2d, slot_t, ye)


def moe_layer(x2d, aff, nffn, wg, wu, wd):
    t, d = x2d.shape
    ne = N_EXPERTS
    cap = CAPACITY_FACTOR * t // ne
    nr = t // LANES
    tm = min(512, cap)
    aff_t = aff[:, :ne].T.reshape(ne, nr, LANES)
    idx, wts, slot, rstart = select_tokens(aff_t, cap)
    ye = run_experts(idx.reshape(ne * cap // tm, tm), x2d, wts.reshape(ne * cap // tm, tm // LANES, LANES), nffn,
                     wg, wu, wd, cap, tm)
    slot_t = slot.reshape(ne, t).T
    rstart_flat = jnp.concatenate([rstart[:, :, 0].T, jnp.full((1, ne), cap, I32)], axis=0).reshape((nr + 1) * ne)
    return combine(x2d, slot_t, rstart_flat, ye, cap)


def _prep_layer(p, l):
    w_in = p["w_in"][l]
    d = D_MODEL
    c_na, c_swa = 3 * d, d + 2 * SWA_KV_HEADS * HEAD_DIM
    o_ssd = c_na + c_swa
    c_ssd = d + (d + 2 * SSD_GROUPS * SSD_STATE)
    o_dt = o_ssd + c_ssd
    o_gate = o_dt + 2 * SSD_HEADS
    w_dt = jnp.pad(w_in[:, o_dt:o_gate], ((0, 0), (0, LANES - 2 * SSD_HEADS)))
    pad_row = lambda v: jnp.pad(v.astype(F32).reshape(-1), (0, LANES - 2 * SSD_HEADS))[None, :]
    r_hi, r_lo = _split2(jnp.pad(p["router_w"][l], ((0, 0), (0, LANES - N_EXPERTS))))
    scale = HEAD_DIM ** -0.5
    head_row = lambda qn, kn, nq, nk: jnp.concatenate(
        [jnp.tile(qn.astype(F32) * scale, nq), jnp.tile(kn.astype(F32), nk)])[None, :]
    w_swa = w_in[:, c_na:o_ssd]
    dup = lambda w: jnp.repeat(w.reshape(d, SWA_KV_HEADS, 1, HEAD_DIM), 2, axis=2).reshape(d, 2 * w.shape[1])
    kv_w = SWA_KV_HEADS * HEAD_DIM
    w_swa = jnp.concatenate([w_swa[:, :d], dup(w_swa[:, d:d + kv_w]), dup(w_swa[:, d + kv_w:])], axis=1)
    return dict(
        norm_mix=p["norm_mix"][l][None, :],
        w_na_in=w_in[:, :c_na].astype(BF16),
        w_swa_in=w_swa.astype(BF16),
        w_ssd_in=w_in[:, o_ssd:o_dt].astype(BF16),
        w_dt=w_dt.astype(BF16),
        w_gate=w_in[:, o_gate:].astype(BF16),
        na_heads=head_row(p["na_q_norm"][l], p["na_k_norm"][l], NA_HEADS, NA_HEADS),
        na_tab=na_bias_table(p["na_rel_bias"][l]),
        swa_heads=head_row(p["swa_q_norm"][l], p["swa_k_norm"][l], SWA_Q_HEADS, 2 * SWA_KV_HEADS),
        swa_sink=p["swa_sink"][l].astype(F32),
        conv_w=p["ssd_conv_w"][l].astype(F32), conv_b=p["ssd_conv_b"][l].astype(F32)[None, :],
        dtb_row=pad_row(p["ssd_dt_bias"][l]), alog_row=pad_row(p["ssd_a_log"][l]),
        dskip=jnp.repeat(p["ssd_d"][l].astype(F32), HEAD_DIM)[None, :],
        ssd_norm=p["ssd_norm"][l].astype(F32)[None, :],
        w_na=p["w_branch_na"][l].astype(BF16), w_swa=p["w_branch_swa"][l].astype(BF16),
        w_ssd=p["w_branch_ssd"][l].astype(BF16), w_out=p["w_out"][l].astype(BF16),
        norm_ffn=p["norm_ffn"][l][None, :], r_hi=r_hi, r_lo=r_lo,
        wg=p["expert_w_gate"][l].astype(BF16), wu=p["expert_w_up"][l].astype(BF16),
        wd=p["expert_w_down"][l].astype(BF16),
    )


def _layer(x, lp, cos, sin):
    bsz, seq, d = x.shape
    t = bsz * seq
    x2d = x.reshape(t, d)
    qkv_na = norm_proj_heads(x2d, lp["norm_mix"], lp["w_na_in"], lp["na_heads"], seq)
    qkv_swa = norm_proj_heads(x2d, lp["norm_mix"], lp["w_swa_in"], lp["swa_heads"], seq, cos, sin)
    zxbc, dt = norm_proj(x2d, lp["norm_mix"], [lp["w_ssd_in"], lp["w_dt"]], [BF16, F32])
    o_na = neighborhood_attention(qkv_na.reshape(bsz, seq, -1), lp["na_tab"])
    o_swa = sliding_window_attention(qkv_swa.reshape(bsz, seq, -1), lp["swa_sink"])
    o_ssd = ssd_mixer(zxbc.reshape(bsz, seq, -1), dt.reshape(bsz, seq, -1), lp["conv_w"], lp["conv_b"],
                      lp["dtb_row"], lp["alog_row"], lp["dskip"], lp["ssd_norm"])
    x_mid, aff = merge_and_route(x2d, lp["norm_mix"], lp["w_gate"], o_na.reshape(t, d), o_swa.reshape(t, d),
                                 o_ssd.reshape(t, d), lp["w_na"], lp["w_swa"], lp["w_ssd"], lp["w_out"],
                                 lp["norm_ffn"], lp["r_hi"], lp["r_lo"])
    return moe_layer(x_mid, aff, lp["norm_ffn"], lp["wg"], lp["wu"], lp["wd"]).reshape(bsz, seq, d)


def kernel(x_prompt, x_sample, norm_mix, w_in, na_q_norm, na_k_norm, na_rel_bias, swa_q_norm, swa_k_norm, swa_sink, ssd_conv_w, ssd_conv_b, ssd_dt_bias, ssd_a_log, ssd_d, ssd_norm, w_branch_na, w_branch_swa, w_branch_ssd, w_out, norm_ffn, router_w, expert_w_gate, expert_w_up, expert_w_down):
    p = dict(norm_mix=norm_mix, w_in=w_in, na_q_norm=na_q_norm, na_k_norm=na_k_norm, na_rel_bias=na_rel_bias,
             swa_q_norm=swa_q_norm, swa_k_norm=swa_k_norm, swa_sink=swa_sink, ssd_conv_w=ssd_conv_w,
             ssd_conv_b=ssd_conv_b, ssd_dt_bias=ssd_dt_bias, ssd_a_log=ssd_a_log, ssd_d=ssd_d, ssd_norm=ssd_norm,
             w_branch_na=w_branch_na, w_branch_swa=w_branch_swa, w_branch_ssd=w_branch_ssd, w_out=w_out,
             norm_ffn=norm_ffn, router_w=router_w, expert_w_gate=expert_w_gate, expert_w_up=expert_w_up,
             expert_w_down=expert_w_down)
    depth = w_in.shape[0]
    layers = [_prep_layer(p, l) for l in range(depth)]
    outs = []
    for x in (x_prompt, x_sample):
        cos, sin = rope_tables(x.shape[1])
        for lp in layers:
            x = _layer(x, lp, cos, sin)
        outs.append(x)
    return tuple(outs)
```
